```python
import jax, jax.numpy as jnp
from jax import lax
import numpy as np

D_MODEL = 1024
BATCH = 8
SEQ = 2048
DEPTH = 1

CHUNK = 64
D_MIX = 2 * D_MODEL
HEAD_DIM = 64
ATT_HEADS = D_MIX // 4 // HEAD_DIM
D_ATT = ATT_HEADS * HEAD_DIM
D_SSD = D_MIX - D_ATT
SSD_HEADS = D_SSD // HEAD_DIM
SSD_GROUPS = 4
SSD_STATE = 128
CONV_WIDTH = 4
CONV_DIM = D_SSD + 2 * SSD_GROUPS * SSD_STATE
IN_PROJ_DIM = D_SSD + CONV_DIM + SSD_HEADS + 3 * D_ATT
N_PREV_CHUNKS = 8
BAND = (N_PREV_CHUNKS + 1) * CHUNK
REL_CLIP = 256
MOE_GROUPS = 4
EXPERTS_PER_GROUP = 8
N_EXPERTS = MOE_GROUPS * EXPERTS_PER_GROUP
TOP_K = 2
D_EXPERT = 512
MOE_BLOCK = 128
EPS = 1e-6

kernel_name = "hymba_ssd_chunkattn_hmoe_adaln"


def rmsnorm(x, g):
    xf = x.astype(jnp.float32)
    xf = xf * lax.rsqrt(jnp.mean(xf * xf, axis=-1, keepdims=True) + EPS)
    return xf.astype(x.dtype) * g


def causal_depthwise_conv(u, w, b):
    out = lax.conv_general_dilated(u, w[:, None, :].astype(u.dtype), window_strides=(1,),
                                   padding=[(CONV_WIDTH - 1, 0)],
                                   dimension_numbers=("NWC", "WIO", "NWC"),
                                   feature_group_count=u.shape[-1])
    return out + b


def ssd_scan(X, A, Bm, Cm):
    b, s, h, p = X.shape
    g, n = Bm.shape[2], Bm.shape[3]
    e = h // g
    nc = s // CHUNK
    X = X.reshape(b, nc, CHUNK, g, e, p)
    A = A.reshape(b, nc, CHUNK, g, e).transpose(0, 1, 3, 4, 2)
    Bm = Bm.reshape(b, nc, CHUNK, g, n)
    Cm = Cm.reshape(b, nc, CHUNK, g, n)
    A_cum = jnp.cumsum(A, axis=-1)
    causal = jnp.tril(jnp.ones((CHUNK, CHUNK), dtype=bool))
    seg = A_cum[..., :, None] - A_cum[..., None, :]
    Lmat = jnp.exp(jnp.where(causal, seg, -jnp.inf))
    CB = jnp.einsum("bclgn,bcsgn->bcgls", Cm, Bm)
    y_diag = jnp.einsum("bcgels,bcsgep->bclgep", CB[:, :, :, None] * Lmat, X)
    decay_states = jnp.exp(A_cum[..., -1:] - A_cum)
    states = jnp.einsum("bclgn,bcgel,bclgep->bcgepn", Bm, decay_states, X)
    chunk_decay = jnp.exp(A_cum[..., -1])

    def step(carry, inp):
        st, dec = inp
        return carry * dec[..., None, None] + st, carry

    init = jnp.zeros((b, g, e, p, n), states.dtype)
    _, prev = lax.scan(step, init, (states.transpose(1, 0, 2, 3, 4, 5),
                                    chunk_decay.transpose(1, 0, 2, 3)))
    prev = prev.transpose(1, 0, 2, 3, 4, 5)
    y_off = jnp.einsum("bclgn,bcgepn,bcgel->bclgep", Cm, prev, jnp.exp(A_cum))
    return (y_diag + y_off).reshape(b, s, h, p)


def ssd_group(z, xbc, dt_raw, conv_w, conv_b, dt_bias, a_log, d_skip, norm_g):
    bsz, s, _ = z.shape
    xbc = jax.nn.silu(causal_depthwise_conv(xbc, conv_w, conv_b))
    xs, Bm, Cm = jnp.split(xbc, [D_SSD, D_SSD + SSD_GROUPS * SSD_STATE], axis=-1)
    xs = xs.reshape(bsz, s, SSD_HEADS, HEAD_DIM).astype(jnp.float32)
    Bm = Bm.reshape(bsz, s, SSD_GROUPS, SSD_STATE).astype(jnp.float32)
    Cm = Cm.reshape(bsz, s, SSD_GROUPS, SSD_STATE).astype(jnp.float32)
    dt = jax.nn.softplus(dt_raw.astype(jnp.float32) + dt_bias.astype(jnp.float32))
    A = -jnp.exp(a_log.astype(jnp.float32))
    y = ssd_scan(xs * dt[..., None], dt * A, Bm, Cm)
    y = y + xs * d_skip.astype(jnp.float32)[:, None]
    y = y.reshape(bsz, s, D_SSD).astype(z.dtype)
    return rmsnorm(y * jax.nn.silu(z), norm_g)


def chunk_band_attention(q, k, v, rel_bias):
    bsz, s, h, dh = q.shape
    nc = s // CHUNK
    pad = N_PREV_CHUNKS * CHUNK
    band_idx = np.arange(nc)[:, None] * CHUNK + np.arange(BAND)[None, :]
    kp = jnp.pad(k, ((0, 0), (pad, 0), (0, 0), (0, 0)))
    vp = jnp.pad(v, ((0, 0), (pad, 0), (0, 0), (0, 0)))
    kb = kp[:, band_idx]
    vb = vp[:, band_idx]
    qc = q.reshape(bsz, nc, CHUNK, h, dh)
    scores = jnp.einsum("bnqhd,bnkhd->bnhqk", qc, kb).astype(jnp.float32) * (dh ** -0.5)
    rel = np.clip(pad + np.arange(CHUNK)[:, None] - np.arange(BAND)[None, :],
                  -REL_CLIP, REL_CLIP) + REL_CLIP
    bias = rel_bias[:, rel].astype(jnp.float32)
    valid = (band_idx - pad) >= 0
    scores = jnp.where(valid[None, :, None, None, :], scores + bias[None, None], -jnp.inf)
    probs = jax.nn.softmax(scores, axis=-1).astype(v.dtype)
    out = jnp.einsum("bnhqk,bnkhd->bnqhd", probs, vb)
    return out.reshape(bsz, s, h * dh)


def hierarchical_moe(h, w_rg, b_rg, w_re, b_re, w1, w3, w2):
    T, d = h.shape
    g_prob = jax.nn.softmax((h @ w_rg + b_rg).astype(jnp.float32), axis=-1)
    g_p, g_idx = lax.top_k(g_prob, 1)
    e_logits = (h @ w_re + b_re).astype(jnp.float32).reshape(T, MOE_GROUPS, EXPERTS_PER_GROUP)
    e_logits = jnp.take_along_axis(e_logits, g_idx[:, :, None], axis=1)[:, 0]
    e_p, e_idx = lax.top_k(jax.nn.softmax(e_logits, axis=-1), TOP_K)
    e_p = e_p / jnp.sum(e_p, axis=-1, keepdims=True)
    gate = g_p * e_p
    expert_id = g_idx * EXPERTS_PER_GROUP + e_idx

    n_assign = T * TOP_K
    flat_e = expert_id.reshape(-1)
    flat_t = jnp.repeat(jnp.arange(T, dtype=jnp.int32), TOP_K)
    flat_g = gate.reshape(-1)
    order = jnp.argsort(flat_e)
    se = flat_e[order]
    counts = jnp.bincount(flat_e, length=N_EXPERTS)
    offsets = jnp.cumsum(counts) - counts
    padded = ((counts + MOE_BLOCK - 1) // MOE_BLOCK) * MOE_BLOCK
    pad_ends = jnp.cumsum(padded)
    pad_offsets = pad_ends - padded
    dest = pad_offsets[se] + (jnp.arange(n_assign) - offsets[se])
    n_blocks = n_assign // MOE_BLOCK + N_EXPERTS
    cap = n_blocks * MOE_BLOCK
    slot_tok = jnp.full((cap,), T, jnp.int32).at[dest].set(flat_t[order])
    slot_gate = jnp.zeros((cap,), h.dtype).at[dest].set(flat_g[order].astype(h.dtype))
    block_e = jnp.minimum(jnp.searchsorted(pad_ends, jnp.arange(n_blocks) * MOE_BLOCK, side="right"),
                          N_EXPERTS - 1).astype(jnp.int32)
    h_pad = jnp.concatenate([h, jnp.zeros((1, d), h.dtype)], axis=0)
    xb = h_pad[slot_tok].reshape(n_blocks, MOE_BLOCK, d)

    def expert_block(args):
        xblk, e = args
        return (jax.nn.silu(xblk @ w1[e]) * (xblk @ w3[e])) @ w2[e]

    yb = lax.map(expert_block, (xb, block_e)).reshape(cap, d) * slot_gate[:, None]
    return jnp.zeros((T + 1, d), h.dtype).at[slot_tok].add(yb)[:T]


def setup_inputs(seed: int = 0) -> dict:
    key = jax.random.key(seed)
    ks = jax.random.split(key, 26)
    f32 = jnp.float32

    def nrm(k, shape, scale):
        return jax.random.normal(k, shape, f32) * scale

    dt = jnp.exp(jax.random.uniform(ks[8], (DEPTH, SSD_HEADS), f32,
                                    minval=float(np.log(1e-3)), maxval=float(np.log(1e-1))))
    return {
        "x": nrm(ks[0], (BATCH, SEQ, D_MODEL), 1.0),
        "c": nrm(ks[1], (BATCH, D_MODEL), 1.0),
        "w_ada": nrm(ks[2], (DEPTH, D_MODEL, 6 * D_MODEL), D_MODEL ** -0.5),
        "b_ada": nrm(ks[3], (DEPTH, 6 * D_MODEL), 0.02),
        "norm_mix_g": 1.0 + nrm(ks[4], (DEPTH, D_MODEL), 0.02),
        "w_in": nrm(ks[5], (DEPTH, D_MODEL, IN_PROJ_DIM), D_MODEL ** -0.5),
        "conv_w": nrm(ks[6], (DEPTH, CONV_WIDTH, CONV_DIM), CONV_WIDTH ** -0.5),
        "conv_b": nrm(ks[7], (DEPTH, CONV_DIM), 0.02),
        "dt_bias": dt + jnp.log(-jnp.expm1(-dt)),
        "a_log": jnp.log(jax.random.uniform(ks[9], (DEPTH, SSD_HEADS), f32, minval=1.0, maxval=16.0)),
        "d_skip": 1.0 + nrm(ks[10], (DEPTH, SSD_HEADS), 0.02),
        "ssd_norm_g": 1.0 + nrm(ks[11], (DEPTH, D_SSD), 0.02),
        "rel_bias": nrm(ks[12], (DEPTH, ATT_HEADS, 2 * REL_CLIP + 1), 0.1),
        "attn_norm_g": 1.0 + nrm(ks[13], (DEPTH, D_ATT), 0.02),
        "w_out": nrm(ks[14], (DEPTH, D_MIX, D_MODEL), D_MIX ** -0.5),
        "norm_ffn_g": 1.0 + nrm(ks[15], (DEPTH, D_MODEL), 0.02),
        "w_router_group": nrm(ks[16], (DEPTH, D_MODEL, MOE_GROUPS), D_MODEL ** -0.5),
        "b_router_group": nrm(ks[17], (DEPTH, MOE_GROUPS), 0.01),
        "w_router_expert": nrm(ks[18], (DEPTH, D_MODEL, N_EXPERTS), D_MODEL ** -0.5),
        "b_router_expert": nrm(ks[19], (DEPTH, N_EXPERTS), 0.01),
        "w1": nrm(ks[20], (DEPTH, N_EXPERTS, D_MODEL, D_EXPERT), D_MODEL ** -0.5),
        "w3": nrm(ks[21], (DEPTH, N_EXPERTS, D_MODEL, D_EXPERT), D_MODEL ** -0.5),
        "w2": nrm(ks[22], (DEPTH, N_EXPERTS, D_EXPERT, D_MODEL), D_EXPERT ** -0.5),
        "final_norm_g": 1.0 + nrm(ks[23], (D_MODEL,), 0.02),
    }


def reference(x, c, w_ada, b_ada, norm_mix_g, w_in, conv_w, conv_b, dt_bias, a_log, d_skip,
              ssd_norm_g, rel_bias, attn_norm_g, w_out, norm_ffn_g, w_router_group,
              b_router_group, w_router_expert, b_router_expert, w1, w3, w2, final_norm_g):
    bsz, s, d = x.shape
    split_pts = np.cumsum([D_SSD, CONV_DIM, SSD_HEADS, D_ATT, D_ATT]).tolist()
    for l in range(DEPTH):
        mod = (jax.nn.silu(c) @ w_ada[l] + b_ada[l])[:, None, :]
        sh_m, sc_m, gt_m, sh_f, sc_f, gt_f = jnp.split(mod, 6, axis=-1)

        h = rmsnorm(x, norm_mix_g[l]) * (1.0 + sc_m) + sh_m
        proj = h @ w_in[l]
        z, xbc, dt_raw, q, k, v = jnp.split(proj, split_pts, axis=-1)
        y_ssd = ssd_group(z, xbc, dt_raw, conv_w[l], conv_b[l], dt_bias[l], a_log[l],
                          d_skip[l], ssd_norm_g[l])
        hs = (bsz, s, ATT_HEADS, HEAD_DIM)
        y_att = chunk_band_attention(q.reshape(hs), k.reshape(hs), v.reshape(hs), rel_bias[l])
        y_att = rmsnorm(y_att, attn_norm_g[l])
        y = jnp.concatenate([y_ssd, y_att], axis=-1) @ w_out[l]
        x = x + gt_m * y

        h = rmsnorm(x, norm_ffn_g[l]) * (1.0 + sc_f) + sh_f
        y = hierarchical_moe(h.reshape(bsz * s, d), w_router_group[l], b_router_group[l],
                             w_router_expert[l], b_router_expert[l], w1[l], w3[l], w2[l])
        x = x + gt_f * y.reshape(bsz, s, d)
    return rmsnorm(x, final_norm_g)
```

```python
import functools

import numpy as np
import jax
import jax.numpy as jnp
from jax import lax
from jax.experimental import pallas as pl
from jax.experimental.pallas import tpu as pltpu

D_MODEL = 1024
CHUNK = 64
HEAD_DIM = 64
D_MIX = 2 * D_MODEL
ATT_HEADS = D_MIX // 4 // HEAD_DIM
D_ATT = ATT_HEADS * HEAD_DIM
D_SSD = D_MIX - D_ATT
SSD_HEADS = D_SSD // HEAD_DIM
SSD_GROUPS = 4
HEADS_PER_GROUP = SSD_HEADS // SSD_GROUPS
SSD_STATE = 128
CONV_WIDTH = 4
D_BC = SSD_GROUPS * SSD_STATE
CONV_DIM = D_SSD + 2 * D_BC
N_PREV_CHUNKS = 8
REL_CLIP = 256
MOE_GROUPS = 4
EXPERTS_PER_GROUP = 8
N_EXPERTS = MOE_GROUPS * EXPERTS_PER_GROUP
TOP_K = 2
D_EXPERT = 512
MOE_BLOCK = 128
EPS = 1e-6

LANES = 128
SUBLANES = 8
ROW_TILES = D_MODEL // LANES
NEG = -1e30
VMEM_LIMIT = 56 * 1024 * 1024

F32 = jnp.float32
BF16 = jnp.bfloat16
HIGHEST = lax.Precision.HIGHEST


def _cparams(*sem):
    return pltpu.CompilerParams(dimension_semantics=sem, vmem_limit_bytes=VMEM_LIMIT)


def _const_spec(shape):
    nd = len(shape)
    return pl.BlockSpec(shape, lambda *_: (0,) * nd, pipeline_mode=pl.Buffered(1))


def _sigmoid(u):
    return 1.0 / (1.0 + jnp.exp(-u))


def _silu(u):
    return u * _sigmoid(u)


def _rms(xf):
    return xf * lax.rsqrt(jnp.mean(xf * xf, axis=-1, keepdims=True) + EPS)


def _dot(a, b):
    return jnp.dot(a, b, preferred_element_type=F32)


def _dot_nt(a, b):
    return lax.dot_general(a, b, (((1,), (1,)), ((), ())), preferred_element_type=F32)


def _dot_f32(a, b):
    return jnp.dot(a, b, preferred_element_type=F32, precision=HIGHEST)


def _adaln_kernel(c_ref, w_ref, b_ref, o_ref):
    o_ref[...] = _dot_f32(_silu(c_ref[...]), w_ref[...]) + b_ref[...]


def _adaln_mod(c, w, b):
    bsz, d = c.shape
    n = w.shape[1]
    tn = 1536
    return pl.pallas_call(
        _adaln_kernel,
        grid=(n // tn,),
        in_specs=[pl.BlockSpec((bsz, d), lambda j: (0, 0)),
                  pl.BlockSpec((d, tn), lambda j: (0, j)),
                  pl.BlockSpec((1, tn), lambda j: (0, j))],
        out_specs=pl.BlockSpec((bsz, tn), lambda j: (0, j)),
        out_shape=jax.ShapeDtypeStruct((bsz, n), F32),
        compiler_params=_cparams("arbitrary"),
    )(c, w, b.reshape(1, n))


IN_SEGS = (("z", D_SSD), ("xbc", CONV_DIM), ("q", D_ATT), ("k", D_ATT), ("v", D_ATT))
IN_WIDTH = sum(n for _, n in IN_SEGS)


def _in_proj_kernel(x_ref, mod_ref, g_ref, w_ref, wdt_ref, z_ref, xbc_ref, q_ref, k_ref, v_ref, dt_ref):
    xf = x_ref[0]
    h = _rms(xf) * g_ref[...] * (1.0 + mod_ref[0, 1:2, :]) + mod_ref[0, 0:1, :]
    hb = h.astype(BF16)
    off = 0
    for (_, n), o_ref in zip(IN_SEGS, (z_ref, xbc_ref, q_ref, k_ref, v_ref)):
        o_ref[...] = _dot(hb, w_ref[:, off:off + n]).astype(o_ref.dtype)
        off += n
    dt_ref[...] = _dot_f32(h, wdt_ref[...])


def _in_proj(x, mod, g, w_cat, w_dt, tm):
    bsz, s, d = x.shape
    t = bsz * s
    spb = s // tm
    row = lambda i: (i, 0)
    outs = [jax.ShapeDtypeStruct((t, n), BF16) for _, n in IN_SEGS] + [jax.ShapeDtypeStruct((t, LANES), F32)]
    out_specs = [pl.BlockSpec((tm, n), row) for _, n in IN_SEGS] + [pl.BlockSpec((tm, LANES), row)]
    return pl.pallas_call(
        _in_proj_kernel,
        grid=(t // tm,),
        in_specs=[pl.BlockSpec((1, tm, d), lambda i: (i // spb, i % spb, 0)),
                  pl.BlockSpec((1, 6, d), lambda i: (i // spb, 0, 0)),
                  _const_spec((1, d)),
                  _const_spec((d, IN_WIDTH)),
                  _const_spec((d, LANES))],
        out_specs=out_specs,
        out_shape=outs,
        compiler_params=_cparams("arbitrary"),
    )(x, mod, g, w_cat, w_dt)


def _ssd_kernel(xbc_ref, z_ref, dt_ref, cw_ref, cb_ref, dtb_ref, alog_ref, dskip_ref, ng_ref,
                o_ref, ext_ref, state_ref):
    L = CHUNK
    E = HEADS_PER_GROUP
    P = HEAD_DIM

    @pl.when(pl.program_id(1) == 0)
    def _():
        ext_ref[0:SUBLANES, :] = jnp.zeros((SUBLANES, CONV_DIM), F32)
        state_ref[...] = jnp.zeros_like(state_ref)

    ext_ref[SUBLANES:SUBLANES + L, :] = xbc_ref[...].astype(F32)
    u = cb_ref[...]
    for w in range(CONV_WIDTH):
        lo = SUBLANES - (CONV_WIDTH - 1) + w
        u = u + ext_ref[lo:lo + L, :] * cw_ref[w:w + 1, :]
    ext_ref[0:SUBLANES, :] = ext_ref[L:L + SUBLANES, :]
    xc = _silu(u)
    xs = xc[:, :D_SSD]

    dtv = dt_ref[...] + dtb_ref[...]
    dt = jnp.maximum(dtv, 0.0) + jnp.log1p(jnp.exp(-jnp.abs(dtv)))
    a = dt * (-jnp.exp(alog_ref[...]))
    ri = lax.broadcasted_iota(jnp.int32, (L, L), 0)
    ci = lax.broadcasted_iota(jnp.int32, (L, L), 1)
    causal = ri >= ci
    acum = _dot_f32(causal.astype(F32), a)
    acum_t = _dot_f32(a.T, (ri <= ci).astype(F32))
    dt_t = dt.T
    a_last = acum[L - 1:L, :]
    exp_a = jnp.exp(acum)
    dec_dt = jnp.exp(a_last - acum) * dt
    chunk_dec = jnp.exp(a_last)

    def lanes_per_head(m, g):
        return jnp.concatenate(
            [jnp.broadcast_to(m[:, g * E + e:g * E + e + 1], (m.shape[0], P)) for e in range(E)], axis=-1)

    ys = []
    for g in range(SSD_GROUPS):
        bm = xc[:, D_SSD + g * SSD_STATE:D_SSD + (g + 1) * SSD_STATE]
        cm = xc[:, D_SSD + D_BC + g * SSD_STATE:D_SSD + D_BC + (g + 1) * SSD_STATE]
        bm_b = bm.astype(BF16)
        cm_b = cm.astype(BF16)
        cb = _dot_nt(cm_b, bm_b)
        xg = xs[:, g * E * P:(g + 1) * E * P]
        prev = state_ref[g]
        y_off = _dot(cm_b, prev.astype(BF16)) * lanes_per_head(exp_a, g)
        yd = []
        for e in range(E):
            h = g * E + e
            seg = acum[:, h:h + 1] - acum_t[h:h + 1, :]
            m = cb * jnp.exp(jnp.where(causal, seg, NEG)) * dt_t[h:h + 1, :]
            yd.append(_dot(m.astype(BF16), xg[:, e * P:(e + 1) * P].astype(BF16)))
        ys.append(jnp.concatenate(yd, axis=-1) + y_off)
        xdec = (xg * lanes_per_head(dec_dt, g)).astype(BF16)
        state_ref[g] = prev * lanes_per_head(chunk_dec, g) + _dot(bm.T.astype(BF16), xdec)

    y = jnp.concatenate(ys, axis=-1) + xs * dskip_ref[...]
    gated = y * _silu(z_ref[...].astype(F32))
    o_ref[...] = (_rms(gated) * ng_ref[...]).astype(o_ref.dtype)


def _ssd(xbc, z, dt_raw, conv_w, conv_b, dt_bias, a_log, d_skip, norm_g, bsz, s):
    t = bsz * s
    nc = s // CHUNK
    row = lambda b, j: (b * nc + j, 0)
    return pl.pallas_call(
        _ssd_kernel,
        grid=(bsz, nc),
        in_specs=[pl.BlockSpec((CHUNK, CONV_DIM), row),
                  pl.BlockSpec((CHUNK, D_SSD), row),
                  pl.BlockSpec((CHUNK, LANES), row),
                  _const_spec((CONV_WIDTH, CONV_DIM)),
                  _const_spec((1, CONV_DIM)),
                  _const_spec((1, LANES)),
                  _const_spec((1, LANES)),
                  _const_spec((1, D_SSD)),
                  _const_spec((1, D_SSD))],
        out_specs=pl.BlockSpec((CHUNK, D_SSD), row),
        out_shape=jax.ShapeDtypeStruct((t, D_SSD), BF16),
        scratch_shapes=[pltpu.VMEM((CHUNK + SUBLANES, CONV_DIM), F32),
                        pltpu.VMEM((SSD_GROUPS, SSD_STATE, HEADS_PER_GROUP * HEAD_DIM), F32)],
        compiler_params=_cparams("arbitrary", "arbitrary"),
    )(xbc, z, dt_raw, conv_w, conv_b, dt_bias, a_log, d_skip, norm_g)


ATT_QB = 256
ATT_KB = ATT_QB + N_PREV_CHUNKS * CHUNK
ATT_NKB = ATT_KB // ATT_QB


def _attn_kernel(q_ref, k2_ref, k1_ref, k0_ref, v2_ref, v1_ref, v0_ref, bias_ref, g_ref, o_ref):
    i = pl.program_id(1)
    k_all = jnp.concatenate([k2_ref[...], k1_ref[...], k0_ref[...]], axis=0)
    v_all = jnp.concatenate([v2_ref[...], v1_ref[...], v0_ref[...]], axis=0)
    kj = lax.broadcasted_iota(jnp.int32, (ATT_QB, ATT_KB), 1)
    valid = kj >= (ATT_KB - ATT_QB) - i * ATT_QB
    outs = []
    for h in range(ATT_HEADS):
        sl = slice(h * HEAD_DIM, (h + 1) * HEAD_DIM)
        sc = _dot_nt(q_ref[:, sl], k_all[:, sl]) * (HEAD_DIM ** -0.5) + bias_ref[h]
        sc = jnp.where(valid, sc, NEG)
        p = jnp.exp(sc - jnp.max(sc, axis=-1, keepdims=True))
        o = _dot(p.astype(BF16), v_all[:, sl])
        outs.append(o / jnp.sum(p, axis=-1, keepdims=True))
    y = jnp.concatenate(outs, axis=-1)
    o_ref[...] = (_rms(y) * g_ref[...]).astype(o_ref.dtype)


def _attn_bias_table(rel_bias):
    qi = np.arange(ATT_QB)[:, None]
    kj = np.arange(ATT_KB)[None, :]
    dist = (ATT_KB - ATT_QB) + qi - kj
    idx = np.clip(dist, -REL_CLIP, REL_CLIP) + REL_CLIP
    dchunk = (ATT_KB - ATT_QB) // CHUNK + qi // CHUNK - kj // CHUNK
    in_band = (dchunk >= 0) & (dchunk <= N_PREV_CHUNKS)
    return jnp.where(in_band[None], rel_bias[:, idx].astype(F32), NEG)


def _band_attn(q, k, v, bias, norm_g, bsz, s):
    t = bsz * s
    nq = s // ATT_QB
    qmap = lambda b, i: (b * nq + i, 0)

    def kmap(back):
        return lambda b, i: (b * nq + jnp.maximum(i - back, 0), 0)

    kv_specs = [pl.BlockSpec((ATT_QB, D_ATT), kmap(back)) for back in (2, 1, 0)]
    return pl.pallas_call(
        _attn_kernel,
        grid=(bsz, nq),
        in_specs=[pl.BlockSpec((ATT_QB, D_ATT), qmap)] + kv_specs + kv_specs
                 + [_const_spec((ATT_HEADS, ATT_QB, ATT_KB)), _const_spec((1, D_ATT))],
        out_specs=pl.BlockSpec((ATT_QB, D_ATT), qmap),
        out_shape=jax.ShapeDtypeStruct((t, D_ATT), BF16),
        compiler_params=_cparams("arbitrary", "arbitrary"),
    )(q, k, k, k, v, v, v, bias, norm_g)


def _out_router_kernel(ys_ref, ya_ref, x_ref, mod_ref, wa_ref, wb_ref, g_ref, wr_ref, br_ref,
                       x1_ref, h2_ref, ri_ref, rf_ref, cnt_ref, run_ref):
    tm = x_ref.shape[1]

    @pl.when(pl.program_id(0) == 0)
    def _():
        run_ref[...] = jnp.zeros_like(run_ref)

    y = _dot(ys_ref[...], wa_ref[...]) + _dot(ya_ref[...], wb_ref[...])
    x1 = x_ref[0] + mod_ref[0, 2:3, :] * y
    x1_ref[...] = x1
    h2 = _rms(x1) * g_ref[...] * (1.0 + mod_ref[0, 4:5, :]) + mod_ref[0, 3:4, :]
    for r in range(ROW_TILES):
        h2_ref[:, r, :] = h2[:, r * LANES:(r + 1) * LANES]

    logits = _dot_f32(h2, wr_ref[...]) + br_ref[...]
    lane = lax.broadcasted_iota(jnp.int32, (tm, LANES), 1)

    def first_argmax(vals, mx):
        return jnp.min(jnp.where(vals == mx, lane, LANES), axis=-1, keepdims=True)

    gmask = (lane >= N_EXPERTS) & (lane < N_EXPERTS + MOE_GROUPS)
    gl = jnp.where(gmask, logits, NEG)
    gmax = jnp.max(gl, axis=-1, keepdims=True)
    gexp = jnp.where(gmask, jnp.exp(gl - gmax), 0.0)
    gprob = gexp / jnp.sum(gexp, axis=-1, keepdims=True)
    g_p = jnp.max(gprob, axis=-1, keepdims=True)
    g_idx = first_argmax(jnp.where(gmask, gprob, -1.0), g_p) - N_EXPERTS

    emask = (lane >= g_idx * EXPERTS_PER_GROUP) & (lane < (g_idx + 1) * EXPERTS_PER_GROUP)
    el = jnp.where(emask, logits, NEG)
    emax = jnp.max(el, axis=-1, keepdims=True)
    eexp = jnp.where(emask, jnp.exp(el - emax), 0.0)
    eprob = jnp.where(emask, eexp / jnp.sum(eexp, axis=-1, keepdims=True), -1.0)
    p1 = jnp.max(eprob, axis=-1, keepdims=True)
    i1 = first_argmax(eprob, p1)
    eprob2 = jnp.where(lane == i1, -1.0, eprob)
    p2 = jnp.max(eprob2, axis=-1, keepdims=True)
    i2 = first_argmax(eprob2, p2)
    psum = p1 + p2
    gate1 = g_p * (p1 / psum)
    gate2 = g_p * (p2 / psum)

    oh1 = (lane == i1).astype(F32)
    oh2 = (lane == i2).astype(F32)
    both = oh1 + oh2
    ri_ = lax.broadcasted_iota(jnp.int32, (tm, tm), 0)
    ci_ = lax.broadcasted_iota(jnp.int32, (tm, tm), 1)
    before = _dot((ri_ > ci_).astype(BF16), both.astype(BF16)) + run_ref[0:1, :]
    rank1 = jnp.sum(oh1 * before, axis=-1, keepdims=True)
    rank2 = jnp.sum(oh2 * before, axis=-1, keepdims=True)
    run_ref[...] = run_ref[...] + jnp.sum(both, axis=0, keepdims=True)
    cnt_ref[...] = run_ref[...]

    ri_ref[...] = jnp.where(lane == 0, i1, jnp.where(lane == 1, i2, jnp.where(
        lane == 2, rank1.astype(jnp.int32), jnp.where(lane == 3, rank2.astype(jnp.int32), 0))))
    rf_ref[...] = jnp.where(lane == 0, gate1, jnp.where(lane == 1, gate2, 0.0))


def _out_router(y_ssd, y_att, x, mod, w_a, w_b, g, w_r, b_r, tm):
    bsz, s, d = x.shape
    t = bsz * s
    spb = s // tm
    row = lambda i: (i, 0)
    return pl.pallas_call(
        _out_router_kernel,
        grid=(t // tm,),
        in_specs=[pl.BlockSpec((tm, D_SSD), row),
                  pl.BlockSpec((tm, D_ATT), row),
                  pl.BlockSpec((1, tm, d), lambda i: (i // spb, i % spb, 0)),
                  pl.BlockSpec((1, 6, d), lambda i: (i // spb, 0, 0)),
                  _const_spec((D_SSD, d)),
                  _const_spec((D_ATT, d)),
                  _const_spec((1, d)),
                  _const_spec((d, LANES)),
                  _const_spec((1, LANES))],
        out_specs=[pl.BlockSpec((tm, d), row),
                   pl.BlockSpec((tm, ROW_TILES, LANES), lambda i: (i, 0, 0)),
                   pl.BlockSpec((tm, LANES), row),
                   pl.BlockSpec((tm, LANES), row),
                   pl.BlockSpec((SUBLANES, LANES), lambda i: (0, 0))],
        out_shape=[jax.ShapeDtypeStruct((t, d), F32),
                   jax.ShapeDtypeStruct((t, ROW_TILES, LANES), F32),
                   jax.ShapeDtypeStruct((t, LANES), jnp.int32),
                   jax.ShapeDtypeStruct((t, LANES), F32),
                   jax.ShapeDtypeStruct((SUBLANES, LANES), F32)],
        scratch_shapes=[pltpu.VMEM((SUBLANES, LANES), F32)],
        compiler_params=_cparams("arbitrary"),
    )(y_ssd, y_att, x, mod, w_a, w_b, g, w_r, b_r)


def _plan_kernel(ri_ref, cnt_ref, dest_ref, blk_ref, *, n_blocks_pad):
    tm = ri_ref.shape[0]
    cnt = cnt_ref[...]
    padded = jnp.floor((cnt + (MOE_BLOCK - 1)) * (1.0 / MOE_BLOCK)) * MOE_BLOCK
    r_ = lax.broadcasted_iota(jnp.int32, (LANES, LANES), 0)
    c_ = lax.broadcasted_iota(jnp.int32, (LANES, LANES), 1)
    pad_ends = _dot_f32(padded, (r_ <= c_).astype(F32))
    pad_off = (pad_ends - padded)[0:1, :]

    ri = ri_ref[...]
    lane = lax.broadcasted_iota(jnp.int32, (tm, LANES), 1)

    def dest(k):
        off = jnp.sum(jnp.where(lane == ri[:, k:k + 1], pad_off, 0.0), axis=-1, keepdims=True)
        return off.astype(jnp.int32) + ri[:, TOP_K + k:TOP_K + k + 1]

    dest_ref[...] = jnp.where(lane == 0, dest(0), jnp.where(lane == 1, dest(1), 0))

    jrow = lax.broadcasted_iota(jnp.int32, (n_blocks_pad, LANES), 0)
    lane_b = lax.broadcasted_iota(jnp.int32, (n_blocks_pad, LANES), 1)
    starts = (jrow * MOE_BLOCK).astype(F32)
    done = jnp.where((lane_b < N_EXPERTS) & (pad_ends[0:1, :] <= starts), 1, 0)
    blk_e = jnp.minimum(jnp.sum(done, axis=-1, keepdims=True), N_EXPERTS - 1)
    used = (pad_ends[0:1, N_EXPERTS - 1:N_EXPERTS] * (1.0 / MOE_BLOCK)).astype(jnp.int32)
    blk_ref[...] = jnp.where(lane_b == 0, blk_e, jnp.where(lane_b == 1, used, 0))


def _slot_plan(route_i, counts, n_blocks, tm):
    t = route_i.shape[0]
    n_blocks_pad = -(-n_blocks // SUBLANES) * SUBLANES
    return pl.pallas_call(
        functools.partial(_plan_kernel, n_blocks_pad=n_blocks_pad),
        grid=(t // tm,),
        in_specs=[pl.BlockSpec((tm, LANES), lambda i: (i, 0)),
                  pl.BlockSpec((SUBLANES, LANES), lambda i: (0, 0))],
        out_specs=[pl.BlockSpec((tm, LANES), lambda i: (i, 0)),
                   pl.BlockSpec((n_blocks_pad, LANES), lambda i: (0, 0))],
        out_shape=[jax.ShapeDtypeStruct((t, LANES), jnp.int32),
                   jax.ShapeDtypeStruct((n_blocks_pad, LANES), jnp.int32)],
        compiler_params=_cparams("arbitrary"),
    )(route_i, counts)


def _dispatch_kernel(d0_ref, d1_ref, h2_ref, xb_in_ref, xb_ref, sem):
    del xb_in_ref
    tb = d0_ref.shape[0]
    base = pl.program_id(0) * tb

    def row_copy(r, d_ref):
        return pltpu.make_async_copy(h2_ref.at[base + r], xb_ref.at[d_ref[r]], sem)

    def issue(r, carry):
        row_copy(r, d0_ref).start()
        row_copy(r, d1_ref).start()
        return carry

    def drain(r, carry):
        row_copy(r, d0_ref).wait()
        row_copy(r, d1_ref).wait()
        return carry

    lax.fori_loop(0, tb, issue, 0)
    lax.fori_loop(0, tb, drain, 0)


def _dispatch(dest0, dest1, h2_rows, cap, tb):
    t = h2_rows.shape[0]
    xb0 = jnp.zeros((cap, ROW_TILES, LANES), F32)
    smem = lambda: pl.BlockSpec((tb,), lambda i: (i,), memory_space=pltpu.SMEM)
    return pl.pallas_call(
        _dispatch_kernel,
        grid=(t // tb,),
        in_specs=[smem(), smem(),
                  pl.BlockSpec(memory_space=pl.ANY),
                  pl.BlockSpec(memory_space=pl.ANY)],
        out_specs=pl.BlockSpec(memory_space=pl.ANY),
        out_shape=jax.ShapeDtypeStruct((cap, ROW_TILES, LANES), F32),
        scratch_shapes=[pltpu.SemaphoreType.DMA(())],
        input_output_aliases={3: 0},
        compiler_params=_cparams("arbitrary"),
    )(dest0, dest1, h2_rows, xb0)


def _experts_kernel(be_ref, used_ref, xb_ref, w1_ref, w3_ref, w2_ref, yb_ref, w1b, w3b, w2b):
    j = pl.program_id(0)
    prev = be_ref[jnp.maximum(j - 1, 0)]

    @pl.when((j == 0) | (be_ref[j] != prev))
    def _():
        w1b[...] = w1_ref[0].astype(BF16)
        w3b[...] = w3_ref[0].astype(BF16)
        w2b[...] = w2_ref[0].astype(BF16)

    @pl.when(j < used_ref[0])
    def _():
        xr = jnp.concatenate([xb_ref[:, r, :] for r in range(ROW_TILES)], axis=-1).astype(BF16)
        hmid = (_silu(_dot(xr, w1b[...])) * _dot(xr, w3b[...])).astype(BF16)
        y = _dot(hmid, w2b[...])
        for r in range(ROW_TILES):
            yb_ref[:, r, :] = y[:, r * LANES:(r + 1) * LANES]

    @pl.when(j >= used_ref[0])
    def _():
        yb_ref[...] = jnp.zeros_like(yb_ref)


def _experts(block_e, used, xb, w1, w3, w2):
    cap = xb.shape[0]
    n_blocks = cap // MOE_BLOCK
    d = w1.shape[1]
    xmap = lambda j, be, used: (jnp.minimum(j, jnp.maximum(used[0] - 1, 0)), 0, 0)
    wmap = lambda j, be, used: (be[j], 0, 0)
    return pl.pallas_call(
        _experts_kernel,
        grid_spec=pltpu.PrefetchScalarGridSpec(
            num_scalar_prefetch=2,
            grid=(n_blocks,),
            in_specs=[pl.BlockSpec((MOE_BLOCK, ROW_TILES, LANES), xmap),
                      pl.BlockSpec((1, d, D_EXPERT), wmap),
                      pl.BlockSpec((1, d, D_EXPERT), wmap),
                      pl.BlockSpec((1, D_EXPERT, d), wmap)],
            out_specs=pl.BlockSpec((MOE_BLOCK, ROW_TILES, LANES), lambda j, be, used: (j, 0, 0)),
            scratch_shapes=[pltpu.VMEM((d, D_EXPERT), BF16),
                            pltpu.VMEM((d, D_EXPERT), BF16),
                            pltpu.VMEM((D_EXPERT, d), BF16)]),
        out_shape=jax.ShapeDtypeStruct((cap, ROW_TILES, LANES), F32),
        compiler_params=_cparams("arbitrary"),
    )(block_e, used, xb, w1, w3, w2)


def _combine_kernel(d0_ref, d1_ref, yb_ref, x1_ref, rf_ref, mod_ref, g_ref, o_ref, ga_ref, gb_ref, sem):
    tm = x1_ref.shape[0]

    def row_copies(r):
        return (pltpu.make_async_copy(yb_ref.at[d0_ref[r]], ga_ref.at[r], sem),
                pltpu.make_async_copy(yb_ref.at[d1_ref[r]], gb_ref.at[r], sem))

    def issue(r, carry):
        for cp in row_copies(r):
            cp.start()
        return carry

    def drain(r, carry):
        for cp in row_copies(r):
            cp.wait()
        return carry

    lax.fori_loop(0, tm, issue, 0)
    lax.fori_loop(0, tm, drain, 0)

    ya = jnp.concatenate([ga_ref[:, r, :] for r in range(ROW_TILES)], axis=-1)
    yb = jnp.concatenate([gb_ref[:, r, :] for r in range(ROW_TILES)], axis=-1)
    rf = rf_ref[...]
    y = rf[:, 0:1] * ya + rf[:, 1:2] * yb
    x2 = x1_ref[...] + mod_ref[0, 5:6, :] * y
    o_ref[0] = _rms(x2) * g_ref[...]


def _combine(dest0, dest1, yb, x1, route_f, mod, g, bsz, s, tm):
    t, d = x1.shape
    spb = s // tm
    smem = lambda: pl.BlockSpec((tm,), lambda i: (i,), memory_space=pltpu.SMEM)
    return pl.pallas_call(
        _combine_kernel,
        grid=(t // tm,),
        in_specs=[smem(), smem(),
                  pl.BlockSpec(memory_space=pl.ANY),
                  pl.BlockSpec((tm, d), lambda i: (i, 0)),
                  pl.BlockSpec((tm, LANES), lambda i: (i, 0)),
                  pl.BlockSpec((1, 6, d), lambda i: (i // spb, 0, 0)),
                  _const_spec((1, d))],
        out_specs=pl.BlockSpec((1, tm, d), lambda i: (i // spb, i % spb, 0)),
        out_shape=jax.ShapeDtypeStruct((bsz, s, d), F32),
        scratch_shapes=[pltpu.VMEM((tm, ROW_TILES, LANES), F32),
                        pltpu.VMEM((tm, ROW_TILES, LANES), F32),
                        pltpu.SemaphoreType.DMA(())],
        compiler_params=_cparams("arbitrary"),
    )(dest0, dest1, yb, x1, route_f, mod, g)


def _pad_lanes(v, fill=0.0):
    return jnp.pad(v, ((0, 0), (0, LANES - v.shape[-1])), constant_values=fill)


def kernel(x, c, w_ada, b_ada, norm_mix_g, w_in, conv_w, conv_b, dt_bias, a_log, d_skip, ssd_norm_g, rel_bias,
           attn_norm_g, w_out, norm_ffn_g, w_router_group, b_router_group, w_router_expert, b_router_expert,
           w1, w3, w2, final_norm_g):
    bsz, s, d = x.shape
    assert d == D_MODEL and w_ada.shape[0] == 1 and s % 512 == 0
    t = bsz * s
    l = 0

    mod = _adaln_mod(c, w_ada[l], b_ada[l]).reshape(bsz, 6, d)

    o_z, o_xbc, o_dt, o_q, o_k = np.cumsum([D_SSD, CONV_DIM, SSD_HEADS, D_ATT, D_ATT]).tolist()
    wi = w_in[l]
    w_cat = jnp.concatenate([wi[:, :o_xbc], wi[:, o_dt:]], axis=1).astype(BF16)
    w_dt = _pad_lanes(wi[:, o_xbc:o_dt])
    z, xbc, q, k, v, dt_raw = _in_proj(x, mod, norm_mix_g[l][None], w_cat, w_dt, tm=512)

    y_ssd = _ssd(xbc, z, dt_raw, conv_w[l], conv_b[l][None], _pad_lanes(dt_bias[l][None]),
                 _pad_lanes(a_log[l][None]), jnp.repeat(d_skip[l], HEAD_DIM)[None], ssd_norm_g[l][None], bsz, s)
    y_att = _band_attn(q, k, v, _attn_bias_table(rel_bias[l]), attn_norm_g[l][None], bsz, s)

    wo = w_out[l].astype(BF16)
    w_r = _pad_lanes(jnp.concatenate([w_router_expert[l], w_router_group[l]], axis=1))
    b_r = _pad_lanes(jnp.concatenate([b_router_expert[l], b_router_group[l]])[None])
    x1, h2_rows, route_i, route_f, counts = _out_router(
        y_ssd, y_att, x, mod, wo[:D_SSD], wo[D_SSD:], norm_ffn_g[l][None], w_r, b_r, tm=256)

    n_blocks = t * TOP_K // MOE_BLOCK + N_EXPERTS
    dest, blk = _slot_plan(route_i, counts, n_blocks, tm=2048)
    dest0, dest1 = dest[:, 0], dest[:, 1]
    block_e, used = blk[:n_blocks, 0], blk[:1, 1]

    xb = _dispatch(dest0, dest1, h2_rows, n_blocks * MOE_BLOCK, tb=1024)
    yb = _experts(block_e, used, xb, w1[l], w3[l], w2[l])
    return _combine(dest0, dest1, yb, x1, route_f, mod, final_norm_g[None], bsz, s, tm=256)
```

```python
import functools

import numpy as np
import jax
import jax.numpy as jnp
from jax import lax
from jax.experimental import pallas as pl
from jax.experimental.pallas import tpu as pltpu

D_MODEL = 1024
CHUNK = 64
HEAD_DIM = 64
D_MIX = 2 * D_MODEL
ATT_HEADS = D_MIX // 4 // HEAD_DIM
D_ATT = ATT_HEADS * HEAD_DIM
D_SSD = D_MIX - D_ATT
SSD_HEADS = D_SSD // HEAD_DIM
SSD_GROUPS = 4
HEADS_PER_GROUP = SSD_HEADS // SSD_GROUPS
SSD_STATE = 128
CONV_WIDTH = 4
D_BC = SSD_GROUPS * SSD_STATE
CONV_DIM = D_SSD + 2 * D_BC
N_PREV_CHUNKS = 8
REL_CLIP = 256
MOE_GROUPS = 4
EXPERTS_PER_GROUP = 8
N_EXPERTS = MOE_GROUPS * EXPERTS_PER_GROUP
TOP_K = 2
D_EXPERT = 512
MOE_BLOCK = 128
EPS = 1e-6

LANES = 128
SUBLANES = 8
ROW_TILES = D_MODEL // LANES
NEG = -1e30
VMEM_LIMIT = 56 * 1024 * 1024

F32 = jnp.float32
BF16 = jnp.bfloat16
HIGHEST = lax.Precision.HIGHEST


def _cparams(*sem):
    return pltpu.CompilerParams(dimension_semantics=sem, vmem_limit_bytes=VMEM_LIMIT)


def _const_spec(shape):
    nd = len(shape)
    return pl.BlockSpec(shape, lambda *_: (0,) * nd, pipeline_mode=pl.Buffered(1))


def _sigmoid(u):
    return 1.0 / (1.0 + jnp.exp(-u))


def _silu(u):
    return u * _sigmoid(u)


def _rms(xf):
    return xf * lax.rsqrt(jnp.mean(xf * xf, axis=-1, keepdims=True) + EPS)


def _dot(a, b):
    return jnp.dot(a, b, preferred_element_type=F32)


def _dot_nt(a, b):
    return lax.dot_general(a, b, (((1,), (1,)), ((), ())), preferred_element_type=F32)


def _dot_f32(a, b):
    return jnp.dot(a, b, preferred_element_type=F32, precision=HIGHEST)


def _adaln_kernel(c_ref, w_ref, b_ref, o_ref):
    o_ref[...] = _dot_f32(_silu(c_ref[...]), w_ref[...]) + b_ref[...]


def _adaln_mod(c, w, b):
    bsz, d = c.shape
    n = w.shape[1]
    tn = 1536
    return pl.pallas_call(
        _adaln_kernel,
        grid=(n // tn,),
        in_specs=[pl.BlockSpec((bsz, d), lambda j: (0, 0)),
                  pl.BlockSpec((d, tn), lambda j: (0, j)),
                  pl.BlockSpec((1, tn), lambda j: (0, j))],
        out_specs=pl.BlockSpec((bsz, tn), lambda j: (0, j)),
        out_shape=jax.ShapeDtypeStruct((bsz, n), F32),
        compiler_params=_cparams("arbitrary"),
    )(c, w, b.reshape(1, n))


IN_SEGS = (("z", D_SSD), ("xbc", CONV_DIM), ("q", D_ATT), ("k", D_ATT), ("v", D_ATT))
IN_WIDTH = sum(n for _, n in IN_SEGS)


def _in_proj_kernel(x_ref, mod_ref, g_ref, w_ref, wdt_ref, z_ref, xbc_ref, q_ref, k_ref, v_ref, dt_ref):
    xf = x_ref[0]
    h = _rms(xf) * g_ref[...] * (1.0 + mod_ref[0, 1:2, :]) + mod_ref[0, 0:1, :]
    hb = h.astype(BF16)
    off = 0
    for (_, n), o_ref in zip(IN_SEGS, (z_ref, xbc_ref, q_ref, k_ref, v_ref)):
        o_ref[...] = _dot(hb, w_ref[:, off:off + n]).astype(o_ref.dtype)
        off += n
    dt_ref[...] = _dot_f32(h, wdt_ref[...])


def _in_proj(x, mod, g, w_cat, w_dt, tm):
    bsz, s, d = x.shape
    t = bsz * s
    spb = s // tm
    row = lambda i: (i, 0)
    outs = [jax.ShapeDtypeStruct((t, n), BF16) for _, n in IN_SEGS] + [jax.ShapeDtypeStruct((t, LANES), F32)]
    out_specs = [pl.BlockSpec((tm, n), row) for _, n in IN_SEGS] + [pl.BlockSpec((tm, LANES), row)]
    return pl.pallas_call(
        _in_proj_kernel,
        grid=(t // tm,),
        in_specs=[pl.BlockSpec((1, tm, d), lambda i: (i // spb, i % spb, 0)),
                  pl.BlockSpec((1, 6, d), lambda i: (i // spb, 0, 0)),
                  _const_spec((1, d)),
                  _const_spec((d, IN_WIDTH)),
                  _const_spec((d, LANES))],
        out_specs=out_specs,
        out_shape=outs,
        compiler_params=_cparams("arbitrary"),
    )(x, mod, g, w_cat, w_dt)


def _ssd_kernel(xbc_ref, z_ref, dt_ref, cw_ref, cb_ref, dtb_ref, alog_ref, dskip_ref, ng_ref,
                o_ref, ext_ref, state_ref):
    L = CHUNK
    E = HEADS_PER_GROUP
    P = HEAD_DIM

    @pl.when(pl.program_id(1) == 0)
    def _():
        ext_ref[0:SUBLANES, :] = jnp.zeros((SUBLANES, CONV_DIM), F32)
        state_ref[...] = jnp.zeros_like(state_ref)

    ext_ref[SUBLANES:SUBLANES + L, :] = xbc_ref[...].astype(F32)
    u = cb_ref[...]
    for w in range(CONV_WIDTH):
        lo = SUBLANES - (CONV_WIDTH - 1) + w
        u = u + ext_ref[lo:lo + L, :] * cw_ref[w:w + 1, :]
    ext_ref[0:SUBLANES, :] = ext_ref[L:L + SUBLANES, :]
    xc = _silu(u)
    xs = xc[:, :D_SSD]

    dtv = dt_ref[...] + dtb_ref[...]
    dt = jnp.maximum(dtv, 0.0) + jnp.log1p(jnp.exp(-jnp.abs(dtv)))
    a = dt * (-jnp.exp(alog_ref[...]))
    ri = lax.broadcasted_iota(jnp.int32, (L, L), 0)
    ci = lax.broadcasted_iota(jnp.int32, (L, L), 1)
    causal = ri >= ci
    acum = _dot_f32(causal.astype(F32), a)
    acum_t = _dot_f32(a.T, (ri <= ci).astype(F32))
    dt_t = dt.T
    a_last = acum[L - 1:L, :]
    exp_a = jnp.exp(acum)
    dec_dt = jnp.exp(a_last - acum) * dt
    chunk_dec = jnp.exp(a_last)

    def lanes_per_head(m, g):
        return jnp.concatenate(
            [jnp.broadcast_to(m[:, g * E + e:g * E + e + 1], (m.shape[0], P)) for e in range(E)], axis=-1)

    ys = []
    for g in range(SSD_GROUPS):
        bm = xc[:, D_SSD + g * SSD_STATE:D_SSD + (g + 1) * SSD_STATE]
        cm = xc[:, D_SSD + D_BC + g * SSD_STATE:D_SSD + D_BC + (g + 1) * SSD_STATE]
        bm_b = bm.astype(BF16)
        cm_b = cm.astype(BF16)
        cb = _dot_nt(cm_b, bm_b)
        xg = xs[:, g * E * P:(g + 1) * E * P]
        prev = state_ref[g]
        y_off = _dot(cm_b, prev.astype(BF16)) * lanes_per_head(exp_a, g)
        yd = []
        for e in range(E):
            h = g * E + e
            seg = acum[:, h:h + 1] - acum_t[h:h + 1, :]
            m = cb * jnp.exp(jnp.where(causal, seg, NEG)) * dt_t[h:h + 1, :]
            yd.append(_dot(m.astype(BF16), xg[:, e * P:(e + 1) * P].astype(BF16)))
        ys.append(jnp.concatenate(yd, axis=-1) + y_off)
        xdec = (xg * lanes_per_head(dec_dt, g)).astype(BF16)
        state_ref[g] = prev * lanes_per_head(chunk_dec, g) + _dot(bm.T.astype(BF16), xdec)

    y = jnp.concatenate(ys, axis=-1) + xs * dskip_ref[...]
    gated = y * _silu(z_ref[...].astype(F32))
    o_ref[...] = (_rms(gated) * ng_ref[...]).astype(o_ref.dtype)


def _ssd(xbc, z, dt_raw, conv_w, conv_b, dt_bias, a_log, d_skip, norm_g, bsz, s):
    t = bsz * s
    nc = s // CHUNK
    row = lambda b, j: (b * nc + j, 0)
    return pl.pallas_call(
        _ssd_kernel,
        grid=(bsz, nc),
        in_specs=[pl.BlockSpec((CHUNK, CONV_DIM), row),
                  pl.BlockSpec((CHUNK, D_SSD), row),
                  pl.BlockSpec((CHUNK, LANES), row),
                  _const_spec((CONV_WIDTH, CONV_DIM)),
                  _const_spec((1, CONV_DIM)),
                  _const_spec((1, LANES)),
                  _const_spec((1, LANES)),
                  _const_spec((1, D_SSD)),
                  _const_spec((1, D_SSD))],
        out_specs=pl.BlockSpec((CHUNK, D_SSD), row),
        out_shape=jax.ShapeDtypeStruct((t, D_SSD), BF16),
        scratch_shapes=[pltpu.VMEM((CHUNK + SUBLANES, CONV_DIM), F32),
                        pltpu.VMEM((SSD_GROUPS, SSD_STATE, HEADS_PER_GROUP * HEAD_DIM), F32)],
        compiler_params=_cparams("arbitrary", "arbitrary"),
    )(xbc, z, dt_raw, conv_w, conv_b, dt_bias, a_log, d_skip, norm_g)


ATT_QB = 256
ATT_KB = ATT_QB + N_PREV_CHUNKS * CHUNK
ATT_NKB = ATT_KB // ATT_QB


def _attn_kernel(q_ref, k2_ref, k1_ref, k0_ref, v2_ref, v1_ref, v0_ref, bias_ref, g_ref, o_ref):
    i = pl.program_id(1)
    k_all = jnp.concatenate([k2_ref[...], k1_ref[...], k0_ref[...]], axis=0)
    v_all = jnp.concatenate([v2_ref[...], v1_ref[...], v0_ref[...]], axis=0)
    kj = lax.broadcasted_iota(jnp.int32, (ATT_QB, ATT_KB), 1)
    valid = kj >= (ATT_KB - ATT_QB) - i * ATT_QB
    outs = []
    for h in range(ATT_HEADS):
        sl = slice(h * HEAD_DIM, (h + 1) * HEAD_DIM)
        sc = _dot_nt(q_ref[:, sl], k_all[:, sl]) * (HEAD_DIM ** -0.5) + bias_ref[h]
        sc = jnp.where(valid, sc, NEG)
        p = jnp.exp(sc - jnp.max(sc, axis=-1, keepdims=True))
        o = _dot(p.astype(BF16), v_all[:, sl])
        outs.append(o / jnp.sum(p, axis=-1, keepdims=True))
    y = jnp.concatenate(outs, axis=-1)
    o_ref[...] = (_rms(y) * g_ref[...]).astype(o_ref.dtype)


def _attn_bias_table(rel_bias):
    nh = rel_bias.shape[0]
    period = ATT_QB + ATT_KB
    back = ATT_KB - ATT_QB
    n_clip_far = back - REL_CLIP + ATT_QB - 1
    assert n_clip_far + 2 * REL_CLIP + 1 == period
    by_offset = jnp.concatenate([jnp.broadcast_to(rel_bias[:, 2 * REL_CLIP:], (nh, n_clip_far)),
                                 rel_bias[:, ::-1]], axis=1).astype(F32)
    cyc = jnp.roll(by_offset, -(ATT_QB - 1), axis=1)
    skew = jnp.tile(cyc, (1, ATT_QB))[:, :ATT_QB * (period - 1)].reshape(nh, ATT_QB, period - 1)
    qi = np.arange(ATT_QB)[:, None]
    kj = np.arange(ATT_KB)[None, :]
    dchunk = back // CHUNK + qi // CHUNK - kj // CHUNK
    in_band = (dchunk >= 0) & (dchunk <= N_PREV_CHUNKS)
    return jnp.where(in_band[None], skew[:, :, :ATT_KB], NEG)


def _band_attn(q, k, v, bias, norm_g, bsz, s):
    t = bsz * s
    nq = s // ATT_QB
    qmap = lambda b, i: (b * nq + i, 0)

    def kmap(back):
        return lambda b, i: (b * nq + jnp.maximum(i - back, 0), 0)

    kv_specs = [pl.BlockSpec((ATT_QB, D_ATT), kmap(back)) for back in (2, 1, 0)]
    return pl.pallas_call(
        _attn_kernel,
        grid=(bsz, nq),
        in_specs=[pl.BlockSpec((ATT_QB, D_ATT), qmap)] + kv_specs + kv_specs
                 + [_const_spec((ATT_HEADS, ATT_QB, ATT_KB)), _const_spec((1, D_ATT))],
        out_specs=pl.BlockSpec((ATT_QB, D_ATT), qmap),
        out_shape=jax.ShapeDtypeStruct((t, D_ATT), BF16),
        compiler_params=_cparams("arbitrary", "arbitrary"),
    )(q, k, k, k, v, v, v, bias, norm_g)


def _out_router_kernel(ys_ref, ya_ref, x_ref, mod_ref, wa_ref, wb_ref, g_ref, wr_ref, br_ref,
                       x1_ref, h2_ref, ri_ref, rf_ref, cnt_ref, run_ref):
    tm = x_ref.shape[1]

    @pl.when(pl.program_id(0) == 0)
    def _():
        run_ref[...] = jnp.zeros_like(run_ref)

    y = _dot(ys_ref[...], wa_ref[...]) + _dot(ya_ref[...], wb_ref[...])
    x1 = x_ref[0] + mod_ref[0, 2:3, :] * y
    x1_ref[...] = x1
    h2 = _rms(x1) * g_ref[...] * (1.0 + mod_ref[0, 4:5, :]) + mod_ref[0, 3:4, :]
    for r in range(ROW_TILES):
        h2_ref[:, r, :] = h2[:, r * LANES:(r + 1) * LANES]

    logits = _dot_f32(h2, wr_ref[...]) + br_ref[...]
    lane = lax.broadcasted_iota(jnp.int32, (tm, LANES), 1)

    def first_argmax(vals, mx):
        return jnp.min(jnp.where(vals == mx, lane, LANES), axis=-1, keepdims=True)

    gmask = (lane >= N_EXPERTS) & (lane < N_EXPERTS + MOE_GROUPS)
    gl = jnp.where(gmask, logits, NEG)
    gmax = jnp.max(gl, axis=-1, keepdims=True)
    gexp = jnp.where(gmask, jnp.exp(gl - gmax), 0.0)
    gprob = gexp / jnp.sum(gexp, axis=-1, keepdims=True)
    g_p = jnp.max(gprob, axis=-1, keepdims=True)
    g_idx = first_argmax(jnp.where(gmask, gprob, -1.0), g_p) - N_EXPERTS

    emask = (lane >= g_idx * EXPERTS_PER_GROUP) & (lane < (g_idx + 1) * EXPERTS_PER_GROUP)
    el = jnp.where(emask, logits, NEG)
    emax = jnp.max(el, axis=-1, keepdims=True)
    eexp = jnp.where(emask, jnp.exp(el - emax), 0.0)
    eprob = jnp.where(emask, eexp / jnp.sum(eexp, axis=-1, keepdims=True), -1.0)
    p1 = jnp.max(eprob, axis=-1, keepdims=True)
    i1 = first_argmax(eprob, p1)
    eprob2 = jnp.where(lane == i1, -1.0, eprob)
    p2 = jnp.max(eprob2, axis=-1, keepdims=True)
    i2 = first_argmax(eprob2, p2)
    psum = p1 + p2
    gate1 = g_p * (p1 / psum)
    gate2 = g_p * (p2 / psum)

    oh1 = (lane == i1).astype(F32)
    oh2 = (lane == i2).astype(F32)
    both = oh1 + oh2
    ri_ = lax.broadcasted_iota(jnp.int32, (tm, tm), 0)
    ci_ = lax.broadcasted_iota(jnp.int32, (tm, tm), 1)
    before = _dot((ri_ > ci_).astype(BF16), both.astype(BF16)) + run_ref[0:1, :]
    rank1 = jnp.sum(oh1 * before, axis=-1, keepdims=True)
    rank2 = jnp.sum(oh2 * before, axis=-1, keepdims=True)
    run_ref[...] = run_ref[...] + jnp.sum(both, axis=0, keepdims=True)
    cnt_ref[...] = run_ref[...]

    ri_ref[...] = jnp.where(lane == 0, i1, jnp.where(lane == 1, i2, jnp.where(
        lane == 2, rank1.astype(jnp.int32), jnp.where(lane == 3, rank2.astype(jnp.int32), 0))))
    rf_ref[...] = jnp.where(lane == 0, gate1, jnp.where(lane == 1, gate2, 0.0))


def _out_router(y_ssd, y_att, x, mod, w_a, w_b, g, w_r, b_r, tm):
    bsz, s, d = x.shape
    t = bsz * s
    spb = s // tm
    row = lambda i: (i, 0)
    return pl.pallas_call(
        _out_router_kernel,
        grid=(t // tm,),
        in_specs=[pl.BlockSpec((tm, D_SSD), row),
                  pl.BlockSpec((tm, D_ATT), row),
                  pl.BlockSpec((1, tm, d), lambda i: (i // spb, i % spb, 0)),
                  pl.BlockSpec((1, 6, d), lambda i: (i // spb, 0, 0)),
                  _const_spec((D_SSD, d)),
                  _const_spec((D_ATT, d)),
                  _const_spec((1, d)),
                  _const_spec((d, LANES)),
                  _const_spec((1, LANES))],
        out_specs=[pl.BlockSpec((tm, d), row),
                   pl.BlockSpec((tm, ROW_TILES, LANES), lambda i: (i, 0, 0)),
                   pl.BlockSpec((tm, LANES), row),
                   pl.BlockSpec((tm, LANES), row),
                   pl.BlockSpec((SUBLANES, LANES), lambda i: (0, 0))],
        out_shape=[jax.ShapeDtypeStruct((t, d), F32),
                   jax.ShapeDtypeStruct((t, ROW_TILES, LANES), F32),
                   jax.ShapeDtypeStruct((t, LANES), jnp.int32),
                   jax.ShapeDtypeStruct((t, LANES), F32),
                   jax.ShapeDtypeStruct((SUBLANES, LANES), F32)],
        scratch_shapes=[pltpu.VMEM((SUBLANES, LANES), F32)],
        compiler_params=_cparams("arbitrary"),
    )(y_ssd, y_att, x, mod, w_a, w_b, g, w_r, b_r)


def _plan_kernel(ri_ref, cnt_ref, dest_ref, blk_ref, *, n_blocks_pad):
    tm = ri_ref.shape[0]
    cnt = cnt_ref[...]
    padded = jnp.floor((cnt + (MOE_BLOCK - 1)) * (1.0 / MOE_BLOCK)) * MOE_BLOCK
    r_ = lax.broadcasted_iota(jnp.int32, (LANES, LANES), 0)
    c_ = lax.broadcasted_iota(jnp.int32, (LANES, LANES), 1)
    pad_ends = _dot_f32(padded, (r_ <= c_).astype(F32))
    pad_off = (pad_ends - padded)[0:1, :]

    ri = ri_ref[...]
    lane = lax.broadcasted_iota(jnp.int32, (tm, LANES), 1)

    def dest(k):
        off = jnp.sum(jnp.where(lane == ri[:, k:k + 1], pad_off, 0.0), axis=-1, keepdims=True)
        return off.astype(jnp.int32) + ri[:, TOP_K + k:TOP_K + k + 1]

    dest_ref[...] = jnp.where(lane == 0, dest(0), jnp.where(lane == 1, dest(1), 0))

    jrow = lax.broadcasted_iota(jnp.int32, (n_blocks_pad, LANES), 0)
    lane_b = lax.broadcasted_iota(jnp.int32, (n_blocks_pad, LANES), 1)
    starts = (jrow * MOE_BLOCK).astype(F32)
    done = jnp.where((lane_b < N_EXPERTS) & (pad_ends[0:1, :] <= starts), 1, 0)
    blk_e = jnp.minimum(jnp.sum(done, axis=-1, keepdims=True), N_EXPERTS - 1)
    used = (pad_ends[0:1, N_EXPERTS - 1:N_EXPERTS] * (1.0 / MOE_BLOCK)).astype(jnp.int32)
    blk_ref[...] = jnp.where(lane_b == 0, blk_e, jnp.where(lane_b == 1, used, 0))


def _slot_plan(route_i, counts, n_blocks, tm):
    t = route_i.shape[0]
    n_blocks_pad = -(-n_blocks // SUBLANES) * SUBLANES
    return pl.pallas_call(
        functools.partial(_plan_kernel, n_blocks_pad=n_blocks_pad),
        grid=(t // tm,),
        in_specs=[pl.BlockSpec((tm, LANES), lambda i: (i, 0)),
                  pl.BlockSpec((SUBLANES, LANES), lambda i: (0, 0))],
        out_specs=[pl.BlockSpec((tm, LANES), lambda i: (i, 0)),
                   pl.BlockSpec((n_blocks_pad, LANES), lambda i: (0, 0))],
        out_shape=[jax.ShapeDtypeStruct((t, LANES), jnp.int32),
                   jax.ShapeDtypeStruct((n_blocks_pad, LANES), jnp.int32)],
        compiler_params=_cparams("arbitrary"),
    )(route_i, counts)


def _dispatch_kernel(d0_ref, d1_ref, h2_ref, xb_in_ref, xb_ref, sem):
    del xb_in_ref
    tb = d0_ref.shape[0]

    def row_copy(r, d_ref):
        return pltpu.make_async_copy(h2_ref.at[r], xb_ref.at[d_ref[r]], sem)

    def issue(r, carry):
        row_copy(r, d0_ref).start()
        row_copy(r, d1_ref).start()
        return carry

    def drain(r, carry):
        row_copy(r, d0_ref).wait()
        row_copy(r, d1_ref).wait()
        return carry

    lax.fori_loop(0, tb, issue, 0)
    lax.fori_loop(0, tb, drain, 0)


def _dispatch(dest0, dest1, h2_rows, cap, tb):
    t = h2_rows.shape[0]
    xb0 = jnp.zeros((cap, ROW_TILES, LANES), F32)
    smem = lambda: pl.BlockSpec((tb,), lambda i: (i,), memory_space=pltpu.SMEM)
    return pl.pallas_call(
        _dispatch_kernel,
        grid=(t // tb,),
        in_specs=[smem(), smem(),
                  pl.BlockSpec((tb, ROW_TILES, LANES), lambda i: (i, 0, 0)),
                  pl.BlockSpec(memory_space=pl.ANY)],
        out_specs=pl.BlockSpec(memory_space=pl.ANY),
        out_shape=jax.ShapeDtypeStruct((cap, ROW_TILES, LANES), F32),
        scratch_shapes=[pltpu.SemaphoreType.DMA(())],
        input_output_aliases={3: 0},
        compiler_params=_cparams("arbitrary"),
    )(dest0, dest1, h2_rows, xb0)


def _experts_kernel(be_ref, used_ref, xb_ref, w1_ref, w3_ref, w2_ref, yb_ref, w1b, w3b, w2b):
    j = pl.program_id(0)
    prev = be_ref[jnp.maximum(j - 1, 0)]

    @pl.when((j == 0) | (be_ref[j] != prev))
    def _():
        w1b[...] = w1_ref[0].astype(BF16)
        w3b[...] = w3_ref[0].astype(BF16)
        w2b[...] = w2_ref[0].astype(BF16)

    @pl.when(j < used_ref[0])
    def _():
        xr = jnp.concatenate([xb_ref[:, r, :] for r in range(ROW_TILES)], axis=-1).astype(BF16)
        hmid = (_silu(_dot(xr, w1b[...])) * _dot(xr, w3b[...])).astype(BF16)
        y = _dot(hmid, w2b[...])
        for r in range(ROW_TILES):
            yb_ref[:, r, :] = y[:, r * LANES:(r + 1) * LANES]

    @pl.when(j >= used_ref[0])
    def _():
        yb_ref[...] = jnp.zeros_like(yb_ref)


def _experts(block_e, used, xb, w1, w3, w2):
    cap = xb.shape[0]
    n_blocks = cap // MOE_BLOCK
    d = w1.shape[1]
    xmap = lambda j, be, used: (jnp.minimum(j, jnp.maximum(used[0] - 1, 0)), 0, 0)
    wmap = lambda j, be, used: (be[j], 0, 0)
    return pl.pallas_call(
        _experts_kernel,
        grid_spec=pltpu.PrefetchScalarGridSpec(
            num_scalar_prefetch=2,
            grid=(n_blocks,),
            in_specs=[pl.BlockSpec((MOE_BLOCK, ROW_TILES, LANES), xmap),
                      pl.BlockSpec((1, d, D_EXPERT), wmap),
                      pl.BlockSpec((1, d, D_EXPERT), wmap),
                      pl.BlockSpec((1, D_EXPERT, d), wmap)],
            out_specs=pl.BlockSpec((MOE_BLOCK, ROW_TILES, LANES), lambda j, be, used: (j, 0, 0)),
            scratch_shapes=[pltpu.VMEM((d, D_EXPERT), BF16),
                            pltpu.VMEM((d, D_EXPERT), BF16),
                            pltpu.VMEM((D_EXPERT, d), BF16)]),
        out_shape=jax.ShapeDtypeStruct((cap, ROW_TILES, LANES), F32),
        compiler_params=_cparams("arbitrary"),
    )(block_e, used, xb, w1, w3, w2)


def _combine_kernel(d0_ref, d1_ref, yb_ref, x1_ref, rf_ref, mod_ref, g_ref, o_ref, ga_ref, gb_ref, sem):
    tm = x1_ref.shape[0]

    def row_copies(r):
        return (pltpu.make_async_copy(yb_ref.at[d0_ref[r]], ga_ref.at[r], sem),
                pltpu.make_async_copy(yb_ref.at[d1_ref[r]], gb_ref.at[r], sem))

    def issue(r, carry):
        for cp in row_copies(r):
            cp.start()
        return carry

    def drain(r, carry):
        for cp in row_copies(r):
            cp.wait()
        return carry

    lax.fori_loop(0, tm, issue, 0)
    lax.fori_loop(0, tm, drain, 0)

    ya = jnp.concatenate([ga_ref[:, r, :] for r in range(ROW_TILES)], axis=-1)
    yb = jnp.concatenate([gb_ref[:, r, :] for r in range(ROW_TILES)], axis=-1)
    rf = rf_ref[...]
    y = rf[:, 0:1] * ya + rf[:, 1:2] * yb
    x2 = x1_ref[...] + mod_ref[0, 5:6, :] * y
    o_ref[0] = _rms(x2) * g_ref[...]


def _combine(dest0, dest1, yb, x1, route_f, mod, g, bsz, s, tm):
    t, d = x1.shape
    spb = s // tm
    smem = lambda: pl.BlockSpec((tm,), lambda i: (i,), memory_space=pltpu.SMEM)
    return pl.pallas_call(
        _combine_kernel,
        grid=(t // tm,),
        in_specs=[smem(), smem(),
                  pl.BlockSpec(memory_space=pl.ANY),
                  pl.BlockSpec((tm, d), lambda i: (i, 0)),
                  pl.BlockSpec((tm, LANES), lambda i: (i, 0)),
                  pl.BlockSpec((1, 6, d), lambda i: (i // spb, 0, 0)),
                  _const_spec((1, d))],
        out_specs=pl.BlockSpec((1, tm, d), lambda i: (i // spb, i % spb, 0)),
        out_shape=jax.ShapeDtypeStruct((bsz, s, d), F32),
        scratch_shapes=[pltpu.VMEM((tm, ROW_TILES, LANES), F32),
                        pltpu.VMEM((tm, ROW_TILES, LANES), F32),
                        pltpu.SemaphoreType.DMA(())],
        compiler_params=_cparams("arbitrary"),
    )(dest0, dest1, yb, x1, route_f, mod, g)


def _pad_lanes(v, fill=0.0):
    return jnp.pad(v, ((0, 0), (0, LANES - v.shape[-1])), constant_values=fill)


def kernel(x, c, w_ada, b_ada, norm_mix_g, w_in, conv_w, conv_b, dt_bias, a_log, d_skip, ssd_norm_g, rel_bias,
           attn_norm_g, w_out, norm_ffn_g, w_router_group, b_router_group, w_router_expert, b_router_expert,
           w1, w3, w2, final_norm_g):
    bsz, s, d = x.shape
    assert d == D_MODEL and w_ada.shape[0] == 1 and s % 512 == 0
    t = bsz * s
    l = 0

    mod = _adaln_mod(c, w_ada[l], b_ada[l]).reshape(bsz, 6, d)

    o_z, o_xbc, o_dt, o_q, o_k = np.cumsum([D_SSD, CONV_DIM, SSD_HEADS, D_ATT, D_ATT]).tolist()
    wi = w_in[l]
    w_cat = jnp.concatenate([wi[:, :o_xbc], wi[:, o_dt:]], axis=1).astype(BF16)
    w_dt = _pad_lanes(wi[:, o_xbc:o_dt])
    z, xbc, q, k, v, dt_raw = _in_proj(x, mod, norm_mix_g[l][None], w_cat, w_dt, tm=512)

    y_ssd = _ssd(xbc, z, dt_raw, conv_w[l], conv_b[l][None], _pad_lanes(dt_bias[l][None]),
                 _pad_lanes(a_log[l][None]), jnp.repeat(d_skip[l], HEAD_DIM)[None], ssd_norm_g[l][None], bsz, s)
    y_att = _band_attn(q, k, v, _attn_bias_table(rel_bias[l]), attn_norm_g[l][None], bsz, s)

    wo = w_out[l].astype(BF16)
    w_r = _pad_lanes(jnp.concatenate([w_router_expert[l], w_router_group[l]], axis=1))
    b_r = _pad_lanes(jnp.concatenate([b_router_expert[l], b_router_group[l]])[None])
    x1, h2_rows, route_i, route_f, counts = _out_router(
        y_ssd, y_att, x, mod, wo[:D_SSD], wo[D_SSD:], norm_ffn_g[l][None], w_r, b_r, tm=256)

    n_blocks = t * TOP_K // MOE_BLOCK + N_EXPERTS
    dest, blk = _slot_plan(route_i, counts, n_blocks, tm=2048)
    dest0, dest1 = dest[:, 0], dest[:, 1]
    block_e, used = blk[:n_blocks, 0], blk[:1, 1]

    xb = _dispatch(dest0, dest1, h2_rows, n_blocks * MOE_BLOCK, tb=512)
    yb = _experts(block_e, used, xb, w1[l], w3[l], w2[l])
    return _combine(dest0, dest1, yb, x1, route_f, mod, final_norm_g[None], bsz, s, tm=256)
```

```python
import numpy as np
import jax
import jax.numpy as jnp
from jax import lax
from jax.experimental import pallas as pl
from jax.experimental.pallas import tpu as pltpu

D_MODEL = 1024
CHUNK = 64
HEAD_DIM = 64
D_MIX = 2 * D_MODEL
ATT_HEADS = D_MIX // 4 // HEAD_DIM
D_ATT = ATT_HEADS * HEAD_DIM
D_SSD = D_MIX - D_ATT
SSD_HEADS = D_SSD // HEAD_DIM
SSD_GROUPS = 4
HEADS_PER_GROUP = SSD_HEADS // SSD_GROUPS
SSD_STATE = 128
CONV_WIDTH = 4
D_BC = SSD_GROUPS * SSD_STATE
CONV_DIM = D_SSD + 2 * D_BC
N_PREV_CHUNKS = 8
REL_CLIP = 256
MOE_GROUPS = 4
EXPERTS_PER_GROUP = 8
N_EXPERTS = MOE_GROUPS * EXPERTS_PER_GROUP
TOP_K = 2
D_EXPERT = 512
SLOT_BLOCK = 256
EPS = 1e-6

LANES = 128
SUBLANES = 8
DMA_UNROLL = 8
NEG = -1e30
VMEM_LIMIT = 56 * 1024 * 1024

F32 = jnp.float32
BF16 = jnp.bfloat16
HIGHEST = lax.Precision.HIGHEST


def _cparams(*sem):
    return pltpu.CompilerParams(dimension_semantics=sem, vmem_limit_bytes=VMEM_LIMIT)


def _const_spec(shape):
    nd = len(shape)
    return pl.BlockSpec(shape, lambda *_: (0,) * nd, pipeline_mode=pl.Buffered(1))


def _sigmoid(u):
    return 1.0 / (1.0 + jnp.exp(-u))


def _silu(u):
    return u * _sigmoid(u)


def _rms(xf):
    return xf * lax.rsqrt(jnp.mean(xf * xf, axis=-1, keepdims=True) + EPS)


def _dot(a, b):
    return jnp.dot(a, b, preferred_element_type=F32)


def _dot_nt(a, b):
    return lax.dot_general(a, b, (((1,), (1,)), ((), ())), preferred_element_type=F32)


def _dot_f32(a, b):
    return jnp.dot(a, b, preferred_element_type=F32, precision=HIGHEST)


def _adaln_kernel(c_ref, w_ref, b_ref, o_ref):
    o_ref[...] = _dot_f32(_silu(c_ref[...]), w_ref[...]) + b_ref[...]


def _adaln_mod(c, w, b):
    bsz, d = c.shape
    n = w.shape[1]
    tn = 1536
    return pl.pallas_call(
        _adaln_kernel,
        grid=(n // tn,),
        in_specs=[pl.BlockSpec((bsz, d), lambda j: (0, 0)),
                  pl.BlockSpec((d, tn), lambda j: (0, j)),
                  pl.BlockSpec((1, tn), lambda j: (0, j))],
        out_specs=pl.BlockSpec((bsz, tn), lambda j: (0, j)),
        out_shape=jax.ShapeDtypeStruct((bsz, n), F32),
        compiler_params=_cparams("arbitrary"),
    )(c, w, b.reshape(1, n))


IN_SEGS = (("z", D_SSD), ("xbc", CONV_DIM), ("q", D_ATT), ("k", D_ATT), ("v", D_ATT))
IN_WIDTH = sum(n for _, n in IN_SEGS)


def _in_proj_kernel(x_ref, mod_ref, g_ref, w_ref, wdt_ref, z_ref, xbc_ref, q_ref, k_ref, v_ref, dt_ref):
    xf = x_ref[0]
    h = _rms(xf) * g_ref[...] * (1.0 + mod_ref[0, 1:2, :]) + mod_ref[0, 0:1, :]
    hb = h.astype(BF16)
    off = 0
    for (_, n), o_ref in zip(IN_SEGS, (z_ref, xbc_ref, q_ref, k_ref, v_ref)):
        o_ref[...] = _dot(hb, w_ref[:, off:off + n]).astype(o_ref.dtype)
        off += n
    dt_ref[...] = _dot_f32(h, wdt_ref[...])


def _in_proj(x, mod, g, w_cat, w_dt, tm):
    bsz, s, d = x.shape
    t = bsz * s
    spb = s // tm
    row = lambda i: (i, 0)
    outs = [jax.ShapeDtypeStruct((t, n), BF16) for _, n in IN_SEGS] + [jax.ShapeDtypeStruct((t, LANES), F32)]
    out_specs = [pl.BlockSpec((tm, n), row) for _, n in IN_SEGS] + [pl.BlockSpec((tm, LANES), row)]
    return pl.pallas_call(
        _in_proj_kernel,
        grid=(t // tm,),
        in_specs=[pl.BlockSpec((1, tm, d), lambda i: (i // spb, i % spb, 0)),
                  pl.BlockSpec((1, 6, d), lambda i: (i // spb, 0, 0)),
                  _const_spec((1, d)),
                  _const_spec((d, IN_WIDTH)),
                  _const_spec((d, LANES))],
        out_specs=out_specs,
        out_shape=outs,
        compiler_params=_cparams("arbitrary"),
    )(x, mod, g, w_cat, w_dt)


def _ssd_kernel(xbc_ref, z_ref, dt_ref, cw_ref, cb_ref, dtb_ref, alog_ref, dskip_ref, ng_ref,
                o_ref, ext_ref, state_ref):
    L = CHUNK
    P = HEAD_DIM
    GW = HEADS_PER_GROUP * P
    assert L == P

    @pl.when(pl.program_id(1) == 0)
    def _():
        ext_ref[0:SUBLANES, :] = jnp.zeros((SUBLANES, CONV_DIM), F32)
        state_ref[...] = jnp.zeros_like(state_ref)

    ext_ref[SUBLANES:SUBLANES + L, :] = xbc_ref[...].astype(F32)
    u = cb_ref[...]
    for w in range(CONV_WIDTH):
        lo = SUBLANES - (CONV_WIDTH - 1) + w
        u = u + ext_ref[lo:lo + L, :] * cw_ref[w:w + 1, :]
    ext_ref[0:SUBLANES, :] = ext_ref[L:L + SUBLANES, :]
    xc = _silu(u)
    xs = xc[:, :D_SSD]

    def per_head_lanes(m):
        return jnp.concatenate([jnp.broadcast_to(m[:, h:h + 1], (L, P)) for h in range(SSD_HEADS)], axis=-1)

    dtv = dt_ref[...] + dtb_ref[...]
    dt = jnp.maximum(dtv, 0.0) + jnp.log1p(jnp.exp(-jnp.abs(dtv)))
    a = dt * (-jnp.exp(alog_ref[...]))
    ri = lax.broadcasted_iota(jnp.int32, (L, L), 0)
    ci = lax.broadcasted_iota(jnp.int32, (L, L), 1)
    acum = _dot_f32((ri >= ci).astype(F32), a)
    acum_t = acum.T
    acum_x = per_head_lanes(acum)
    acum_keys = jnp.concatenate([acum_t[h:h + 1, :] for h in range(SSD_HEADS)], axis=-1)
    dt_x = per_head_lanes(dt)
    li = lax.broadcasted_iota(jnp.int32, (L, D_SSD), 0)
    si = lax.broadcasted_iota(jnp.int32, (L, D_SSD), 1) & (P - 1)
    decay_ls = jnp.exp(jnp.where(li >= si, acum_x - acum_keys, NEG))
    a_last = acum_x[L - 1:L, :]
    exp_a = jnp.exp(acum_x)
    chunk_dec = jnp.exp(a_last)
    xdt = xs * dt_x
    xdt_b = xdt.astype(BF16)
    xdec_b = (xdt * jnp.exp(a_last - acum_x)).astype(BF16)
    lane2 = lax.broadcasted_iota(jnp.int32, (L, 2 * P), 1)
    zero_b = jnp.zeros((L, 2 * P), BF16)

    ys = []
    for g in range(SSD_GROUPS):
        bm = xc[:, D_SSD + g * SSD_STATE:D_SSD + (g + 1) * SSD_STATE]
        cm_b = xc[:, D_SSD + D_BC + g * SSD_STATE:D_SSD + D_BC + (g + 1) * SSD_STATE].astype(BF16)
        cb = _dot_nt(cm_b, bm.astype(BF16))
        cb2 = jnp.concatenate([cb, cb], axis=-1)
        gl = slice(g * GW, (g + 1) * GW)
        prev = state_ref[g]
        yg = []
        for j in range(GW // (2 * P)):
            pl_ = slice(g * GW + j * 2 * P, g * GW + (j + 1) * 2 * P)
            m2 = (cb2 * decay_ls[:, pl_]).astype(BF16)
            x2 = xdt_b[:, pl_]
            rhs = jnp.concatenate([jnp.where(lane2 < P, x2, zero_b), jnp.where(lane2 >= P, x2, zero_b)], axis=0)
            yg.append(_dot(m2, rhs))
        ys.append(jnp.concatenate(yg, axis=-1) + _dot(cm_b, prev.astype(BF16)) * exp_a[:, gl])
        state_ref[g] = prev * chunk_dec[:, gl] + _dot(bm.T.astype(BF16), xdec_b[:, gl])

    y = jnp.concatenate(ys, axis=-1) + xs * dskip_ref[...]
    gated = y * _silu(z_ref[...].astype(F32))
    o_ref[...] = (_rms(gated) * ng_ref[...]).astype(o_ref.dtype)


def _ssd(xbc, z, dt_raw, conv_w, conv_b, dt_bias, a_log, d_skip, norm_g, bsz, s):
    t = bsz * s
    nc = s // CHUNK
    row = lambda b, j: (b * nc + j, 0)
    return pl.pallas_call(
        _ssd_kernel,
        grid=(bsz, nc),
        in_specs=[pl.BlockSpec((CHUNK, CONV_DIM), row),
                  pl.BlockSpec((CHUNK, D_SSD), row),
                  pl.BlockSpec((CHUNK, LANES), row),
                  _const_spec((CONV_WIDTH, CONV_DIM)),
                  _const_spec((1, CONV_DIM)),
                  _const_spec((1, LANES)),
                  _const_spec((1, LANES)),
                  _const_spec((1, D_SSD)),
                  _const_spec((1, D_SSD))],
        out_specs=pl.BlockSpec((CHUNK, D_SSD), row),
        out_shape=jax.ShapeDtypeStruct((t, D_SSD), BF16),
        scratch_shapes=[pltpu.VMEM((CHUNK + SUBLANES, CONV_DIM), F32),
                        pltpu.VMEM((SSD_GROUPS, SSD_STATE, HEADS_PER_GROUP * HEAD_DIM), F32)],
        compiler_params=_cparams("arbitrary", "arbitrary"),
    )(xbc, z, dt_raw, conv_w, conv_b, dt_bias, a_log, d_skip, norm_g)


ATT_QB = 256
ATT_KB = ATT_QB + N_PREV_CHUNKS * CHUNK


def _attn_kernel(q_ref, k2_ref, k1_ref, k0_ref, v2_ref, v1_ref, v0_ref, bias_ref, g_ref, o_ref):
    i = pl.program_id(1)
    k_all = jnp.concatenate([k2_ref[...], k1_ref[...], k0_ref[...]], axis=0)
    v_all = jnp.concatenate([v2_ref[...], v1_ref[...], v0_ref[...]], axis=0)
    kj = lax.broadcasted_iota(jnp.int32, (ATT_QB, ATT_KB), 1)
    valid = kj >= (ATT_KB - ATT_QB) - i * ATT_QB
    outs = []
    for h in range(ATT_HEADS):
        sl = slice(h * HEAD_DIM, (h + 1) * HEAD_DIM)
        sc = _dot_nt(q_ref[:, sl], k_all[:, sl]) * (HEAD_DIM ** -0.5) + bias_ref[h]
        sc = jnp.where(valid, sc, NEG)
        p = jnp.exp(sc - jnp.max(sc, axis=-1, keepdims=True))
        o = _dot(p.astype(BF16), v_all[:, sl])
        outs.append(o / jnp.sum(p, axis=-1, keepdims=True))
    y = jnp.concatenate(outs, axis=-1)
    o_ref[...] = (_rms(y) * g_ref[...]).astype(o_ref.dtype)


def _attn_bias_table(rel_bias):
    nh = rel_bias.shape[0]
    period = ATT_QB + ATT_KB
    back = ATT_KB - ATT_QB
    n_clip_far = back - REL_CLIP + ATT_QB - 1
    assert n_clip_far + 2 * REL_CLIP + 1 == period
    by_offset = jnp.concatenate([jnp.broadcast_to(rel_bias[:, 2 * REL_CLIP:], (nh, n_clip_far)),
                                 rel_bias[:, ::-1]], axis=1).astype(F32)
    cyc = jnp.roll(by_offset, -(ATT_QB - 1), axis=1)
    skew = jnp.tile(cyc, (1, ATT_QB))[:, :ATT_QB * (period - 1)].reshape(nh, ATT_QB, period - 1)
    qi = np.arange(ATT_QB)[:, None]
    kj = np.arange(ATT_KB)[None, :]
    dchunk = back // CHUNK + qi // CHUNK - kj // CHUNK
    in_band = (dchunk >= 0) & (dchunk <= N_PREV_CHUNKS)
    return jnp.where(in_band[None], skew[:, :, :ATT_KB], NEG)


def _band_attn(q, k, v, bias, norm_g, bsz, s):
    t = bsz * s
    nq = s // ATT_QB
    qmap = lambda b, i: (b * nq + i, 0)

    def kmap(back):
        return lambda b, i: (b * nq + jnp.maximum(i - back, 0), 0)

    kv_specs = [pl.BlockSpec((ATT_QB, D_ATT), kmap(back)) for back in (2, 1, 0)]
    return pl.pallas_call(
        _attn_kernel,
        grid=(bsz, nq),
        in_specs=[pl.BlockSpec((ATT_QB, D_ATT), qmap)] + kv_specs + kv_specs
                 + [_const_spec((ATT_HEADS, ATT_QB, ATT_KB)), _const_spec((1, D_ATT))],
        out_specs=pl.BlockSpec((ATT_QB, D_ATT), qmap),
        out_shape=jax.ShapeDtypeStruct((t, D_ATT), BF16),
        compiler_params=_cparams("arbitrary", "arbitrary"),
    )(q, k, k, k, v, v, v, bias, norm_g)


ROUTER_ROWS = 48


def _out_router_kernel(ys_ref, ya_ref, x_ref, mod_ref, wa_ref, wb_ref, g_ref, wrh_ref, wrl_ref, br_ref,
                       x1_ref, h2_ref, ri_ref, rf_ref, cnt_ref, run_ref):
    tm = x_ref.shape[1]

    @pl.when(pl.program_id(0) == 0)
    def _():
        run_ref[...] = jnp.zeros_like(run_ref)

    y = _dot(ys_ref[...], wa_ref[...]) + _dot(ya_ref[...], wb_ref[...])
    x1 = x_ref[0] + mod_ref[0, 2:3, :] * y
    x1_ref[...] = x1
    h2 = _rms(x1) * g_ref[...] * (1.0 + mod_ref[0, 4:5, :]) + mod_ref[0, 3:4, :]
    h2_ref[...] = h2

    h_hi = h2.astype(BF16)
    h_lo = (h2 - h_hi.astype(F32)).astype(BF16)
    lt = (_dot_nt(wrh_ref[...], h_hi) + _dot_nt(wrh_ref[...], h_lo) + _dot_nt(wrl_ref[...], h_hi)) + br_ref[...]

    sub = lax.broadcasted_iota(jnp.int32, (SUBLANES, tm), 0)
    assert EXPERTS_PER_GROUP == SUBLANES and MOE_GROUPS <= SUBLANES

    def first_argmax(vals, mx):
        return jnp.min(jnp.where(vals == mx, sub, SUBLANES), axis=0, keepdims=True)

    gl = jnp.where(sub < MOE_GROUPS, lt[N_EXPERTS:N_EXPERTS + SUBLANES], NEG)
    gexp = jnp.exp(gl - jnp.max(gl, axis=0, keepdims=True))
    gprob = gexp / jnp.sum(gexp, axis=0, keepdims=True)
    g_p = jnp.max(gprob, axis=0, keepdims=True)
    g_idx = first_argmax(gprob, g_p)

    el = lt[(MOE_GROUPS - 1) * SUBLANES:MOE_GROUPS * SUBLANES]
    for g in range(MOE_GROUPS - 2, -1, -1):
        el = jnp.where(g_idx == g, lt[g * SUBLANES:(g + 1) * SUBLANES], el)
    eexp = jnp.exp(el - jnp.max(el, axis=0, keepdims=True))
    eprob = eexp / jnp.sum(eexp, axis=0, keepdims=True)
    p1 = jnp.max(eprob, axis=0, keepdims=True)
    i1 = first_argmax(eprob, p1)
    eprob2 = jnp.where(sub == i1, -1.0, eprob)
    p2 = jnp.max(eprob2, axis=0, keepdims=True)
    i2 = first_argmax(eprob2, p2)
    psum = p1 + p2
    gate1 = g_p * (p1 / psum)
    gate2 = g_p * (p2 / psum)
    e1 = g_idx * EXPERTS_PER_GROUP + i1
    e2 = g_idx * EXPERTS_PER_GROUP + i2

    erow = lax.broadcasted_iota(jnp.int32, (N_EXPERTS, tm), 0)
    oh1 = (erow == e1).astype(F32)
    oh2 = (erow == e2).astype(F32)
    both = oh1 + oh2
    ri_ = lax.broadcasted_iota(jnp.int32, (tm, tm), 0)
    ci_ = lax.broadcasted_iota(jnp.int32, (tm, tm), 1)
    run = run_ref[...]
    before = _dot(both.astype(BF16), (ri_ < ci_).astype(BF16)) + jnp.concatenate([run] * (tm // LANES), axis=1)
    rank1 = jnp.sum(oh1 * before, axis=0, keepdims=True).astype(jnp.int32)
    rank2 = jnp.sum(oh2 * before, axis=0, keepdims=True).astype(jnp.int32)
    run = run + jnp.sum(both, axis=1, keepdims=True)
    run_ref[...] = run
    cnt_ref[...] = run

    ri_ref[...] = jnp.where(sub == 0, e1, jnp.where(sub == 1, e2, jnp.where(sub == 2, rank1, jnp.where(
        sub == 3, rank2, 0))))
    gates = jnp.where(sub == 0, gate1, jnp.where(sub == 1, gate2, 0.0))
    rf_ref[...] = jnp.concatenate([gates, jnp.zeros((LANES - SUBLANES, tm), F32)], axis=0).T


def _out_router(y_ssd, y_att, x, mod, w_a, w_b, g, wr_hi, wr_lo, b_r, tm):
    bsz, s, d = x.shape
    t = bsz * s
    spb = s // tm
    row = lambda i: (i, 0)
    return pl.pallas_call(
        _out_router_kernel,
        grid=(t // tm,),
        in_specs=[pl.BlockSpec((tm, D_SSD), row),
                  pl.BlockSpec((tm, D_ATT), row),
                  pl.BlockSpec((1, tm, d), lambda i: (i // spb, i % spb, 0)),
                  pl.BlockSpec((1, 6, d), lambda i: (i // spb, 0, 0)),
                  _const_spec((D_SSD, d)),
                  _const_spec((D_ATT, d)),
                  _const_spec((1, d)),
                  _const_spec((ROUTER_ROWS, d)),
                  _const_spec((ROUTER_ROWS, d)),
                  _const_spec((ROUTER_ROWS, tm))],
        out_specs=[pl.BlockSpec((tm, d), row),
                   pl.BlockSpec((tm, d), row),
                   pl.BlockSpec((SUBLANES, tm), lambda i: (0, i)),
                   pl.BlockSpec((tm, LANES), row),
                   pl.BlockSpec((N_EXPERTS, LANES), lambda i: (0, 0))],
        out_shape=[jax.ShapeDtypeStruct((t, d), F32),
                   jax.ShapeDtypeStruct((t, d), F32),
                   jax.ShapeDtypeStruct((SUBLANES, t), jnp.int32),
                   jax.ShapeDtypeStruct((t, LANES), F32),
                   jax.ShapeDtypeStruct((N_EXPERTS, LANES), F32)],
        scratch_shapes=[pltpu.VMEM((N_EXPERTS, LANES), F32)],
        compiler_params=_cparams("arbitrary"),
    )(y_ssd, y_att, x, mod, w_a, w_b, g, wr_hi, wr_lo, b_r)


def _plan_kernel(ri_ref, cnt_ref, dest_ref, blk_ref):
    tp = ri_ref.shape[1]
    nbl = blk_ref.shape[1]
    cnt = cnt_ref[...]
    padded = jnp.floor((cnt + (SLOT_BLOCK - 1)) * (1.0 / SLOT_BLOCK)) * SLOT_BLOCK
    r_ = lax.broadcasted_iota(jnp.int32, (N_EXPERTS, N_EXPERTS), 0)
    c_ = lax.broadcasted_iota(jnp.int32, (N_EXPERTS, N_EXPERTS), 1)
    pad_ends = _dot_f32((r_ >= c_).astype(F32), padded)
    pad_off = pad_ends - padded

    ri = ri_ref[...]
    erow = lax.broadcasted_iota(jnp.int32, (N_EXPERTS, tp), 0)
    pad_off_t = jnp.concatenate([pad_off] * (tp // LANES), axis=1)

    def dest(k):
        off = jnp.sum(jnp.where(erow == ri[k:k + 1, :], pad_off_t, 0.0), axis=0, keepdims=True)
        return off.astype(jnp.int32) + ri[TOP_K + k:TOP_K + k + 1, :]

    sub = lax.broadcasted_iota(jnp.int32, (SUBLANES, tp), 0)
    dest_ref[...] = jnp.where(sub == 0, dest(0), jnp.where(sub == 1, dest(1), 0))

    starts = (lax.broadcasted_iota(jnp.int32, (N_EXPERTS, nbl), 1) * SLOT_BLOCK).astype(F32)
    pad_ends_t = jnp.concatenate([pad_ends] * (nbl // LANES), axis=1)
    done = jnp.sum(jnp.where(pad_ends_t <= starts, 1, 0), axis=0, keepdims=True)
    blk_e = jnp.minimum(done, N_EXPERTS - 1)
    used = (pad_ends[N_EXPERTS - 1:N_EXPERTS, 0:1] * (1.0 / SLOT_BLOCK)).astype(jnp.int32)
    sub_b = lax.broadcasted_iota(jnp.int32, (SUBLANES, nbl), 0)
    blk_ref[...] = jnp.where(sub_b == 0, blk_e, jnp.where(sub_b == 1, used, 0))


def _slot_plan(route_i, counts, n_blocks, tp):
    t = route_i.shape[1]
    nbl = -(-n_blocks // LANES) * LANES
    return pl.pallas_call(
        _plan_kernel,
        grid=(t // tp,),
        in_specs=[pl.BlockSpec((SUBLANES, tp), lambda i: (0, i)),
                  pl.BlockSpec((N_EXPERTS, LANES), lambda i: (0, 0))],
        out_specs=[pl.BlockSpec((SUBLANES, tp), lambda i: (0, i)),
                   pl.BlockSpec((SUBLANES, nbl), lambda i: (0, 0))],
        out_shape=[jax.ShapeDtypeStruct((SUBLANES, t), jnp.int32),
                   jax.ShapeDtypeStruct((SUBLANES, nbl), jnp.int32)],
        compiler_params=_cparams("arbitrary"),
    )(route_i, counts)


def _dispatch_kernel(d0_ref, d1_ref, h2_ref, xb_in_ref, xb_ref, sem):
    del xb_in_ref
    tb = d0_ref.shape[0]

    def row_copy(r, d_ref):
        return pltpu.make_async_copy(h2_ref.at[pl.ds(r, 1)], xb_ref.at[pl.ds(d_ref[r], 1)], sem)

    def issue(i, carry):
        for u in range(DMA_UNROLL):
            row_copy(i * DMA_UNROLL + u, d0_ref).start()
            row_copy(i * DMA_UNROLL + u, d1_ref).start()
        return carry

    def drain(i, carry):
        for u in range(DMA_UNROLL):
            row_copy(i * DMA_UNROLL + u, d0_ref).wait()
            row_copy(i * DMA_UNROLL + u, d1_ref).wait()
        return carry

    lax.fori_loop(0, tb // DMA_UNROLL, issue, 0)
    lax.fori_loop(0, tb // DMA_UNROLL, drain, 0)


def _dispatch(dest0, dest1, h2_rows, cap, tb):
    t = h2_rows.shape[0]
    xb0 = jnp.zeros((cap, D_MODEL), F32)
    smem = lambda: pl.BlockSpec((tb,), lambda i: (i,), memory_space=pltpu.SMEM)
    return pl.pallas_call(
        _dispatch_kernel,
        grid=(t // tb,),
        in_specs=[smem(), smem(),
                  pl.BlockSpec((tb, D_MODEL), lambda i: (i, 0)),
                  pl.BlockSpec(memory_space=pl.ANY)],
        out_specs=pl.BlockSpec(memory_space=pl.ANY),
        out_shape=jax.ShapeDtypeStruct((cap, D_MODEL), F32),
        scratch_shapes=[pltpu.SemaphoreType.DMA(())],
        input_output_aliases={3: 0},
        compiler_params=_cparams("arbitrary"),
    )(dest0, dest1, h2_rows, xb0)


def _experts_kernel(be_ref, used_ref, xb_ref, w1_ref, w3_ref, w2_ref, yb_ref, w1b, w3b, w2b):
    j = pl.program_id(0)
    prev = be_ref[jnp.maximum(j - 1, 0)]

    @pl.when((j == 0) | (be_ref[j] != prev))
    def _():
        w1b[...] = w1_ref[0].astype(BF16)
        w3b[...] = w3_ref[0].astype(BF16)
        w2b[...] = w2_ref[0].astype(BF16)

    @pl.when(j < used_ref[0])
    def _():
        xr = xb_ref[...].astype(BF16)
        hmid = (_silu(_dot(xr, w1b[...])) * _dot(xr, w3b[...])).astype(BF16)
        yb_ref[...] = _dot(hmid, w2b[...])

    @pl.when(j >= used_ref[0])
    def _():
        yb_ref[...] = jnp.zeros_like(yb_ref)


def _experts(block_e, used, xb, w1, w3, w2):
    cap = xb.shape[0]
    n_blocks = cap // SLOT_BLOCK
    d = w1.shape[1]
    xmap = lambda j, be, used: (jnp.minimum(j, jnp.maximum(used[0] - 1, 0)), 0)
    wmap = lambda j, be, used: (be[j], 0, 0)
    return pl.pallas_call(
        _experts_kernel,
        grid_spec=pltpu.PrefetchScalarGridSpec(
            num_scalar_prefetch=2,
            grid=(n_blocks,),
            in_specs=[pl.BlockSpec((SLOT_BLOCK, d), xmap),
                      pl.BlockSpec((1, d, D_EXPERT), wmap),
                      pl.BlockSpec((1, d, D_EXPERT), wmap),
                      pl.BlockSpec((1, D_EXPERT, d), wmap)],
            out_specs=pl.BlockSpec((SLOT_BLOCK, d), lambda j, be, used: (j, 0)),
            scratch_shapes=[pltpu.VMEM((d, D_EXPERT), BF16),
                            pltpu.VMEM((d, D_EXPERT), BF16),
                            pltpu.VMEM((D_EXPERT, d), BF16)]),
        out_shape=jax.ShapeDtypeStruct((cap, d), F32),
        compiler_params=_cparams("arbitrary"),
    )(block_e, used, xb, w1, w3, w2)


def _combine_kernel(d0_ref, d1_ref, yb_ref, x1_ref, rf_ref, mod_ref, g_ref, o_ref, ga_ref, gb_ref, sem):
    tm = x1_ref.shape[0]

    def row_copies(r):
        return (pltpu.make_async_copy(yb_ref.at[pl.ds(d0_ref[r], 1)], ga_ref.at[pl.ds(r, 1)], sem),
                pltpu.make_async_copy(yb_ref.at[pl.ds(d1_ref[r], 1)], gb_ref.at[pl.ds(r, 1)], sem))

    def issue(i, carry):
        for u in range(DMA_UNROLL):
            for cp in row_copies(i * DMA_UNROLL + u):
                cp.start()
        return carry

    def drain(i, carry):
        for u in range(DMA_UNROLL):
            for cp in row_copies(i * DMA_UNROLL + u):
                cp.wait()
        return carry

    lax.fori_loop(0, tm // DMA_UNROLL, issue, 0)
    lax.fori_loop(0, tm // DMA_UNROLL, drain, 0)

    rf = rf_ref[...]
    y = rf[:, 0:1] * ga_ref[...] + rf[:, 1:2] * gb_ref[...]
    x2 = x1_ref[...] + mod_ref[0, 5:6, :] * y
    o_ref[0] = _rms(x2) * g_ref[...]


def _combine(dest0, dest1, yb, x1, route_f, mod, g, bsz, s, tm):
    t, d = x1.shape
    spb = s // tm
    smem = lambda: pl.BlockSpec((tm,), lambda i: (i,), memory_space=pltpu.SMEM)
    return pl.pallas_call(
        _combine_kernel,
        grid=(t // tm,),
        in_specs=[smem(), smem(),
                  pl.BlockSpec(memory_space=pl.ANY),
                  pl.BlockSpec((tm, d), lambda i: (i, 0)),
                  pl.BlockSpec((tm, LANES), lambda i: (i, 0)),
                  pl.BlockSpec((1, 6, d), lambda i: (i // spb, 0, 0)),
                  _const_spec((1, d))],
        out_specs=pl.BlockSpec((1, tm, d), lambda i: (i // spb, i % spb, 0)),
        out_shape=jax.ShapeDtypeStruct((bsz, s, d), F32),
        scratch_shapes=[pltpu.VMEM((tm, d), F32),
                        pltpu.VMEM((tm, d), F32),
                        pltpu.SemaphoreType.DMA(())],
        compiler_params=_cparams("arbitrary"),
    )(dest0, dest1, yb, x1, route_f, mod, g)


def _pad_lanes(v):
    return jnp.pad(v, ((0, 0), (0, LANES - v.shape[-1])))


def kernel(x, c, w_ada, b_ada, norm_mix_g, w_in, conv_w, conv_b, dt_bias, a_log, d_skip, ssd_norm_g, rel_bias,
           attn_norm_g, w_out, norm_ffn_g, w_router_group, b_router_group, w_router_expert, b_router_expert,
           w1, w3, w2, final_norm_g):
    bsz, s, d = x.shape
    assert d == D_MODEL and w_ada.shape[0] == 1 and s % 512 == 0
    t = bsz * s
    l = 0

    mod = _adaln_mod(c, w_ada[l], b_ada[l]).reshape(bsz, 6, d)

    o_z, o_xbc, o_dt, o_q, o_k = np.cumsum([D_SSD, CONV_DIM, SSD_HEADS, D_ATT, D_ATT]).tolist()
    wi = w_in[l]
    w_cat = jnp.concatenate([wi[:, :o_xbc], wi[:, o_dt:]], axis=1).astype(BF16)
    w_dt = _pad_lanes(wi[:, o_xbc:o_dt])
    z, xbc, q, k, v, dt_raw = _in_proj(x, mod, norm_mix_g[l][None], w_cat, w_dt, tm=512)

    y_ssd = _ssd(xbc, z, dt_raw, conv_w[l], conv_b[l][None], _pad_lanes(dt_bias[l][None]),
                 _pad_lanes(a_log[l][None]), jnp.repeat(d_skip[l], HEAD_DIM)[None], ssd_norm_g[l][None], bsz, s)
    y_att = _band_attn(q, k, v, _attn_bias_table(rel_bias[l]), attn_norm_g[l][None], bsz, s)

    wo = w_out[l].astype(BF16)
    tm_out = 256
    w_rt = jnp.concatenate([w_router_expert[l].T, w_router_group[l].T,
                            jnp.zeros((ROUTER_ROWS - N_EXPERTS - MOE_GROUPS, d), F32)], axis=0)
    wr_hi = w_rt.astype(BF16)
    wr_lo = (w_rt - wr_hi.astype(F32)).astype(BF16)
    b_rt = jnp.concatenate([b_router_expert[l], b_router_group[l],
                            jnp.zeros((ROUTER_ROWS - N_EXPERTS - MOE_GROUPS,), F32)])
    x1, h2_rows, route_i, route_f, counts = _out_router(
        y_ssd, y_att, x, mod, wo[:D_SSD], wo[D_SSD:], norm_ffn_g[l][None], wr_hi, wr_lo,
        jnp.broadcast_to(b_rt[:, None], (ROUTER_ROWS, tm_out)), tm=tm_out)

    n_blocks = t * TOP_K // SLOT_BLOCK + N_EXPERTS
    dest, blk = _slot_plan(route_i, counts, n_blocks, tp=2048)
    dest0, dest1 = dest[0], dest[1]
    block_e, used = blk[0, :n_blocks], blk[1, :1]

    xb = _dispatch(dest0, dest1, h2_rows, n_blocks * SLOT_BLOCK, tb=512)
    yb = _experts(block_e, used, xb, w1[l], w3[l], w2[l])
    return _combine(dest0, dest1, yb, x1, route_f, mod, final_norm_g[None], bsz, s, tm=256)
```

```python
import functools

import numpy as np
import jax
import jax.numpy as jnp
from jax import lax
from jax.experimental import pallas as pl
from jax.experimental.pallas import tpu as pltpu

D_MODEL = 1024
CHUNK = 64
HEAD_DIM = 64
D_MIX = 2 * D_MODEL
ATT_HEADS = D_MIX // 4 // HEAD_DIM
D_ATT = ATT_HEADS * HEAD_DIM
D_SSD = D_MIX - D_ATT
SSD_HEADS = D_SSD // HEAD_DIM
SSD_GROUPS = 4
HEADS_PER_GROUP = SSD_HEADS // SSD_GROUPS
SSD_STATE = 128
CONV_WIDTH = 4
D_BC = SSD_GROUPS * SSD_STATE
CONV_DIM = D_SSD + 2 * D_BC
N_PREV_CHUNKS = 8
REL_CLIP = 256
MOE_GROUPS = 4
EXPERTS_PER_GROUP = 8
N_EXPERTS = MOE_GROUPS * EXPERTS_PER_GROUP
TOP_K = 2
D_EXPERT = 512
SLOT_BLOCK = 256
EPS = 1e-6

LANES = 128
SUBLANES = 8
DMA_UNROLL = 8
LOG2E = 1.4426950408889634
NEG = -1e30
VMEM_LIMIT = 56 * 1024 * 1024

F32 = jnp.float32
BF16 = jnp.bfloat16
HIGHEST = lax.Precision.HIGHEST


def _cparams(*sem):
    return pltpu.CompilerParams(dimension_semantics=sem, vmem_limit_bytes=VMEM_LIMIT)


def _const_spec(shape):
    nd = len(shape)
    return pl.BlockSpec(shape, lambda *_: (0,) * nd, pipeline_mode=pl.Buffered(1))


def _sigmoid(u):
    return 1.0 / (1.0 + jnp.exp(-u))


def _silu(u):
    return u * _sigmoid(u)


def _rms(xf):
    return xf * lax.rsqrt(jnp.mean(xf * xf, axis=-1, keepdims=True) + EPS)


def _bf16_head(x):
    bits = lax.bitcast_convert_type(x, jnp.uint32) & jnp.uint32(0xFFFF0000)
    return lax.bitcast_convert_type(bits, F32)


def _dot(a, b):
    return jnp.dot(a, b, preferred_element_type=F32)


def _dot_nt(a, b):
    return lax.dot_general(a, b, (((1,), (1,)), ((), ())), preferred_element_type=F32)


def _dot_f32(a, b):
    return jnp.dot(a, b, preferred_element_type=F32, precision=HIGHEST)


def _adaln_kernel(c_ref, w_ref, b_ref, o_ref):
    o_ref[...] = _dot_f32(_silu(c_ref[...]), w_ref[...]) + b_ref[...]


def _adaln_mod(c, w, b):
    bsz, d = c.shape
    n = w.shape[1]
    tn = 1536
    return pl.pallas_call(
        _adaln_kernel,
        grid=(n // tn,),
        in_specs=[pl.BlockSpec((bsz, d), lambda j: (0, 0)),
                  pl.BlockSpec((d, tn), lambda j: (0, j)),
                  pl.BlockSpec((1, tn), lambda j: (0, j))],
        out_specs=pl.BlockSpec((bsz, tn), lambda j: (0, j)),
        out_shape=jax.ShapeDtypeStruct((bsz, n), F32),
        compiler_params=_cparams("arbitrary"),
    )(c, w, b.reshape(1, n))


IN_SEGS = (("z", D_SSD), ("xbc", CONV_DIM), ("q", D_ATT), ("k", D_ATT), ("v", D_ATT))
IN_WIDTH = sum(n for _, n in IN_SEGS)


def _in_proj_kernel(x_ref, mod_ref, g_ref, w_ref, wdt_ref, z_ref, xbc_ref, q_ref, k_ref, v_ref, dt_ref):
    xf = x_ref[0]
    h = _rms(xf) * g_ref[...] * (1.0 + mod_ref[0, 1:2, :]) + mod_ref[0, 0:1, :]
    hb = h.astype(BF16)
    off = 0
    for (name, n), o_ref in zip(IN_SEGS, (z_ref, xbc_ref, q_ref, k_ref, v_ref)):
        acc = _dot(hb, w_ref[:, off:off + n])
        if name == "q":
            acc = acc * (HEAD_DIM ** -0.5 * LOG2E)
        o_ref[...] = acc.astype(o_ref.dtype)
        off += n
    dt_ref[...] = _dot_f32(h, wdt_ref[...])


def _in_proj(x, mod, g, w_cat, w_dt, tm):
    bsz, s, d = x.shape
    t = bsz * s
    spb = s // tm
    row = lambda i: (i, 0)
    outs = [jax.ShapeDtypeStruct((t, n), BF16) for _, n in IN_SEGS] + [jax.ShapeDtypeStruct((t, LANES), F32)]
    out_specs = [pl.BlockSpec((tm, n), row) for _, n in IN_SEGS] + [pl.BlockSpec((tm, LANES), row)]
    return pl.pallas_call(
        _in_proj_kernel,
        grid=(t // tm,),
        in_specs=[pl.BlockSpec((1, tm, d), lambda i: (i // spb, i % spb, 0)),
                  pl.BlockSpec((1, 6, d), lambda i: (i // spb, 0, 0)),
                  _const_spec((1, d)),
                  _const_spec((d, IN_WIDTH)),
                  _const_spec((d, LANES))],
        out_specs=out_specs,
        out_shape=outs,
        compiler_params=_cparams("arbitrary"),
    )(x, mod, g, w_cat, w_dt)


CONV_TAIL = CHUNK
SSD_SEQS = 2


def _conv_shift_matrix():
    sel = np.zeros((CONV_WIDTH * CHUNK, CONV_TAIL + CHUNK), np.float32)
    for w in range(CONV_WIDTH):
        for t in range(CHUNK):
            sel[w * CHUNK + t, CONV_TAIL + t - (CONV_WIDTH - 1) + w] = 1.0
    return jnp.asarray(sel, BF16)


def _head_expand_matrix():
    e = np.zeros((LANES, D_SSD), np.float32)
    for h in range(SSD_HEADS):
        e[h, h * HEAD_DIM:(h + 1) * HEAD_DIM] = 1.0
    return jnp.asarray(e, BF16)


def _ssd_kernel(xbc_ref, z_ref, dt_ref, shift_ref, expand_ref, cw_ref, cb_ref, dtb_ref, alog_ref, dskip_ref, ng_ref,
                o_ref, *carry_refs):
    L = CHUNK
    P = HEAD_DIM
    GW = HEADS_PER_GROUP * P
    assert L == P
    seqs = range(SSD_SEQS)
    tail_refs, state_refs = carry_refs[:SSD_SEQS], carry_refs[SSD_SEQS:]
    j = pl.program_id(1)

    @pl.when(j == 0)
    def _():
        for tail_ref, state_ref in zip(tail_refs, state_refs):
            tail_ref[0] = jnp.zeros((CONV_TAIL, CONV_DIM), BF16)
            state_ref[...] = jnp.zeros_like(state_ref)

    dtv = [dt_ref[b] + dtb_ref[...] for b in seqs]
    dt = [jnp.maximum(v, 0.0) + jnp.log1p(jnp.exp(-jnp.abs(v))) for v in dtv]
    neg_a = -jnp.exp(alog_ref[...])
    ri = lax.broadcasted_iota(jnp.int32, (L, L), 0)
    ci = lax.broadcasted_iota(jnp.int32, (L, L), 1)
    tri = (ri >= ci).astype(F32)
    acum = [_dot_f32(tri, d * neg_a) for d in dt]

    shifted = [_dot(shift_ref[...], jnp.concatenate([tail_refs[b][j % 2], xbc_ref[b]], axis=0))
               for b in seqs]
    for b in seqs:
        tail_refs[b][(j + 1) % 2] = xbc_ref[b, L - CONV_TAIL:L, :]
    xc = []
    for b in seqs:
        u = cb_ref[...]
        for w in range(CONV_WIDTH):
            u = u + shifted[b][w * L:(w + 1) * L, :] * cw_ref[w:w + 1, :]
        xc.append(_silu(u))
    xs = [m[:, :D_SSD] for m in xc]
    bm = [[xc[b][:, D_SSD + g * SSD_STATE:D_SSD + (g + 1) * SSD_STATE] for g in range(SSD_GROUPS)] for b in seqs]
    cm_b = [[xc[b][:, D_SSD + D_BC + g * SSD_STATE:D_SSD + D_BC + (g + 1) * SSD_STATE].astype(BF16)
             for g in range(SSD_GROUPS)] for b in seqs]
    cb = [[_dot_nt(cm_b[b][g], bm[b][g].astype(BF16)) for g in range(SSD_GROUPS)] for b in seqs]

    def bf16_terms(m):
        hi = _bf16_head(m)
        r1 = m - hi
        mid = _bf16_head(r1)
        return [hi.astype(BF16), mid.astype(BF16), (r1 - mid).astype(BF16)]

    ex = [_dot(jnp.concatenate(bf16_terms(dt[b]) + bf16_terms(acum[b]), axis=0), expand_ref[...]) for b in seqs]
    dt_x = [(e[0:L] + e[L:2 * L]) + e[2 * L:3 * L] for e in ex]
    acum_x = [(e[3 * L:4 * L] + e[4 * L:5 * L]) + e[5 * L:6 * L] for e in ex]
    acum_t = [m.T for m in acum]
    acum_keys = [jnp.concatenate([t[h:h + 1, :] for h in range(SSD_HEADS)], axis=-1) for t in acum_t]
    li = lax.broadcasted_iota(jnp.int32, (L, D_SSD), 0)
    si = lax.broadcasted_iota(jnp.int32, (L, D_SSD), 1) & (P - 1)
    decay_ls = [jnp.exp(jnp.where(li >= si, acum_x[b] - acum_keys[b], NEG)) for b in seqs]
    a_last = [m[L - 1:L, :] for m in acum_x]
    exp_a = [jnp.exp(m) for m in acum_x]
    chunk_dec = [jnp.exp(m) for m in a_last]
    xdt = [xs[b] * dt_x[b] for b in seqs]
    xdt_b = [m.astype(BF16) for m in xdt]
    xdec_b = [(xdt[b] * jnp.exp(a_last[b] - acum_x[b])).astype(BF16) for b in seqs]
    lane2 = lax.broadcasted_iota(jnp.int32, (L, 2 * P), 1)
    first_b = (lane2 < P).astype(F32).astype(BF16)
    second_b = (lane2 >= P).astype(F32).astype(BF16)

    ys = [[] for _ in seqs]
    for g in range(SSD_GROUPS):
        gl = slice(g * GW, (g + 1) * GW)
        prev = [state_refs[b][g] for b in seqs]
        y_off = [_dot(cm_b[b][g], prev[b].astype(BF16)) * exp_a[b][:, gl] for b in seqs]
        new_state = [prev[b] * chunk_dec[b][:, gl] + _dot(bm[b][g].T.astype(BF16), xdec_b[b][:, gl]) for b in seqs]
        for b in seqs:
            state_refs[b][g] = new_state[b]
        for b in seqs:
            cb2 = jnp.concatenate([cb[b][g], cb[b][g]], axis=-1)
            yg = []
            for j in range(GW // (2 * P)):
                pl_ = slice(g * GW + j * 2 * P, g * GW + (j + 1) * 2 * P)
                m2 = (cb2 * decay_ls[b][:, pl_]).astype(BF16)
                x2 = xdt_b[b][:, pl_]
                yg.append(_dot(m2, jnp.concatenate([x2 * first_b, x2 * second_b], axis=0)))
            ys[b].append(jnp.concatenate(yg, axis=-1) + y_off[b])

    y = [jnp.concatenate(ys[b], axis=-1) + xs[b] * dskip_ref[...] for b in seqs]
    gated = [y[b] * _silu(z_ref[b].astype(F32)) for b in seqs]
    for b in seqs:
        o_ref[b] = (_rms(gated[b]) * ng_ref[...]).astype(o_ref.dtype)


def _ssd(xbc, z, dt_raw, conv_w, conv_b, dt_bias, a_log, d_skip, norm_g, bsz, s):
    nc = s // CHUNK
    seq = lambda width: pl.BlockSpec((SSD_SEQS, CHUNK, width), lambda b, j: (b, j, 0))
    return pl.pallas_call(
        _ssd_kernel,
        grid=(bsz // SSD_SEQS, nc),
        in_specs=[seq(CONV_DIM), seq(D_SSD), seq(LANES),
                  _const_spec((CONV_WIDTH * CHUNK, CONV_TAIL + CHUNK)),
                  _const_spec((LANES, D_SSD)),
                  _const_spec((CONV_WIDTH, CONV_DIM)),
                  _const_spec((1, CONV_DIM)),
                  _const_spec((1, LANES)),
                  _const_spec((1, LANES)),
                  _const_spec((1, D_SSD)),
                  _const_spec((1, D_SSD))],
        out_specs=seq(D_SSD),
        out_shape=jax.ShapeDtypeStruct((bsz, s, D_SSD), BF16),
        scratch_shapes=[pltpu.VMEM((2, CONV_TAIL, CONV_DIM), BF16)] * SSD_SEQS
                       + [pltpu.VMEM((SSD_GROUPS, SSD_STATE, HEADS_PER_GROUP * HEAD_DIM), F32)] * SSD_SEQS,
        compiler_params=_cparams("arbitrary", "arbitrary"),
    )(xbc.reshape(bsz, s, CONV_DIM), z.reshape(bsz, s, D_SSD), dt_raw.reshape(bsz, s, LANES),
      _conv_shift_matrix(), _head_expand_matrix(), conv_w, conv_b, dt_bias, a_log, d_skip, norm_g).reshape(bsz * s, D_SSD)


ATT_QB = 256
ATT_KB = ATT_QB + N_PREV_CHUNKS * CHUNK


def _attn_kernel(q_ref, k2_ref, k1_ref, k0_ref, v2_ref, v1_ref, v0_ref, bias_ref, g_ref, o_ref):
    i = pl.program_id(1)
    k_refs = (k2_ref, k1_ref, k0_ref)
    v_refs = (v2_ref, v1_ref, v0_ref)
    nkb_max = ATT_KB // ATT_QB

    def attend(nkb):
        lo = (nkb_max - nkb) * ATT_QB
        k_all = jnp.concatenate([r[...] for r in k_refs[nkb_max - nkb:]], axis=0)
        v_all = jnp.concatenate([r[...] for r in v_refs[nkb_max - nkb:]], axis=0)
        outs = []
        for h in range(ATT_HEADS):
            sl = slice(h * HEAD_DIM, (h + 1) * HEAD_DIM)
            sc = _dot_nt(q_ref[:, sl], k_all[:, sl]) + bias_ref[h, :, lo:]
            p = jnp.exp2(sc - jnp.max(sc, axis=-1, keepdims=True))
            o = _dot(p.astype(BF16), v_all[:, sl])
            outs.append(o / jnp.sum(p, axis=-1, keepdims=True))
        y = jnp.concatenate(outs, axis=-1)
        o_ref[...] = (_rms(y) * g_ref[...]).astype(o_ref.dtype)

    for nkb in range(1, nkb_max):
        pl.when(i == nkb - 1)(functools.partial(attend, nkb))
    pl.when(i >= nkb_max - 1)(functools.partial(attend, nkb_max))


def _attn_bias_table(rel_bias):
    nh = rel_bias.shape[0]
    period = ATT_QB + ATT_KB
    back = ATT_KB - ATT_QB
    n_clip_far = back - REL_CLIP + ATT_QB - 1
    assert n_clip_far + 2 * REL_CLIP + 1 == period
    by_offset = jnp.concatenate([jnp.broadcast_to(rel_bias[:, 2 * REL_CLIP:], (nh, n_clip_far)),
                                 rel_bias[:, ::-1]], axis=1).astype(F32)
    cyc = jnp.roll(by_offset, -(ATT_QB - 1), axis=1)
    skew = jnp.tile(cyc, (1, ATT_QB))[:, :ATT_QB * (period - 1)].reshape(nh, ATT_QB, period - 1)
    qi = np.arange(ATT_QB)[:, None]
    kj = np.arange(ATT_KB)[None, :]
    dchunk = back // CHUNK + qi // CHUNK - kj // CHUNK
    in_band = (dchunk >= 0) & (dchunk <= N_PREV_CHUNKS)
    return jnp.where(in_band[None], skew[:, :, :ATT_KB] * LOG2E, NEG)


def _band_attn(q, k, v, bias, norm_g, bsz, s):
    t = bsz * s
    nq = s // ATT_QB
    qmap = lambda b, i: (b * nq + i, 0)

    def kmap(back):
        return lambda b, i: (b * nq + jnp.maximum(i - back, 0), 0)

    kv_specs = [pl.BlockSpec((ATT_QB, D_ATT), kmap(back)) for back in (2, 1, 0)]
    return pl.pallas_call(
        _attn_kernel,
        grid=(bsz, nq),
        in_specs=[pl.BlockSpec((ATT_QB, D_ATT), qmap)] + kv_specs + kv_specs
                 + [_const_spec((ATT_HEADS, ATT_QB, ATT_KB)), _const_spec((1, D_ATT))],
        out_specs=pl.BlockSpec((ATT_QB, D_ATT), qmap),
        out_shape=jax.ShapeDtypeStruct((t, D_ATT), BF16),
        compiler_params=_cparams("arbitrary", "arbitrary"),
    )(q, k, k, k, v, v, v, bias, norm_g)


ROUTER_ROWS = 48


def _out_router_kernel(ys_ref, ya_ref, x_ref, mod_ref, wa_ref, wb_ref, g_ref, wrh_ref, wrl_ref, br_ref,
                       x1_ref, h2_ref, ri_ref, rf_ref, cnt_ref, run_ref):
    tm = x_ref.shape[1]

    @pl.when(pl.program_id(0) == 0)
    def _():
        run_ref[...] = jnp.zeros_like(run_ref)

    y = _dot(ys_ref[...], wa_ref[...]) + _dot(ya_ref[...], wb_ref[...])
    x1 = x_ref[0] + mod_ref[0, 2:3, :] * y
    x1_ref[...] = x1
    h2 = _rms(x1) * g_ref[...] * (1.0 + mod_ref[0, 4:5, :]) + mod_ref[0, 3:4, :]
    h2_ref[...] = h2

    h_head = _bf16_head(h2)
    h_hi = h_head.astype(BF16)
    h_lo = (h2 - h_head).astype(BF16)
    lt = (_dot_nt(wrh_ref[...], h_hi) + _dot_nt(wrh_ref[...], h_lo) + _dot_nt(wrl_ref[...], h_hi)) + br_ref[...]

    sub = lax.broadcasted_iota(jnp.int32, (SUBLANES, tm), 0)
    assert EXPERTS_PER_GROUP == SUBLANES and MOE_GROUPS <= SUBLANES

    def first_argmax(vals, mx):
        return jnp.min(jnp.where(vals == mx, sub, SUBLANES), axis=0, keepdims=True)

    gl = jnp.where(sub < MOE_GROUPS, lt[N_EXPERTS:N_EXPERTS + SUBLANES], NEG)
    gexp = jnp.exp(gl - jnp.max(gl, axis=0, keepdims=True))
    gprob = gexp / jnp.sum(gexp, axis=0, keepdims=True)
    g_p = jnp.max(gprob, axis=0, keepdims=True)
    g_idx = first_argmax(gprob, g_p)

    el = lt[(MOE_GROUPS - 1) * SUBLANES:MOE_GROUPS * SUBLANES]
    for g in range(MOE_GROUPS - 2, -1, -1):
        el = jnp.where(g_idx == g, lt[g * SUBLANES:(g + 1) * SUBLANES], el)
    eexp = jnp.exp(el - jnp.max(el, axis=0, keepdims=True))
    eprob = eexp / jnp.sum(eexp, axis=0, keepdims=True)
    p1 = jnp.max(eprob, axis=0, keepdims=True)
    i1 = first_argmax(eprob, p1)
    eprob2 = jnp.where(sub == i1, -1.0, eprob)
    p2 = jnp.max(eprob2, axis=0, keepdims=True)
    i2 = first_argmax(eprob2, p2)
    psum = p1 + p2
    gate1 = g_p * (p1 / psum)
    gate2 = g_p * (p2 / psum)
    e1 = g_idx * EXPERTS_PER_GROUP + i1
    e2 = g_idx * EXPERTS_PER_GROUP + i2

    erow = lax.broadcasted_iota(jnp.int32, (N_EXPERTS, tm), 0)
    oh1 = (erow == e1).astype(F32)
    oh2 = (erow == e2).astype(F32)
    both = oh1 + oh2
    ri_ = lax.broadcasted_iota(jnp.int32, (tm, tm), 0)
    ci_ = lax.broadcasted_iota(jnp.int32, (tm, tm), 1)
    run = run_ref[...]
    before = _dot(both.astype(BF16), (ri_ < ci_).astype(BF16)) + jnp.concatenate([run] * (tm // LANES), axis=1)
    rank1 = jnp.sum(oh1 * before, axis=0, keepdims=True).astype(jnp.int32)
    rank2 = jnp.sum(oh2 * before, axis=0, keepdims=True).astype(jnp.int32)
    run = run + jnp.sum(both, axis=1, keepdims=True)
    run_ref[...] = run
    cnt_ref[...] = run

    ri_ref[...] = jnp.where(sub == 0, e1, jnp.where(sub == 1, e2, jnp.where(sub == 2, rank1, jnp.where(
        sub == 3, rank2, 0))))
    gates = jnp.where(sub == 0, gate1, jnp.where(sub == 1, gate2, 0.0))
    rf_ref[...] = jnp.concatenate([gates, jnp.zeros((LANES - SUBLANES, tm), F32)], axis=0).T


def _out_router(y_ssd, y_att, x, mod, w_a, w_b, g, wr_hi, wr_lo, b_r, tm):
    bsz, s, d = x.shape
    t = bsz * s
    spb = s // tm
    row = lambda i: (i, 0)
    return pl.pallas_call(
        _out_router_kernel,
        grid=(t // tm,),
        in_specs=[pl.BlockSpec((tm, D_SSD), row),
                  pl.BlockSpec((tm, D_ATT), row),
                  pl.BlockSpec((1, tm, d), lambda i: (i // spb, i % spb, 0)),
                  pl.BlockSpec((1, 6, d), lambda i: (i // spb, 0, 0)),
                  _const_spec((D_SSD, d)),
                  _const_spec((D_ATT, d)),
                  _const_spec((1, d)),
                  _const_spec((ROUTER_ROWS, d)),
                  _const_spec((ROUTER_ROWS, d)),
                  _const_spec((ROUTER_ROWS, tm))],
        out_specs=[pl.BlockSpec((tm, d), row),
                   pl.BlockSpec((tm, d), row),
                   pl.BlockSpec((SUBLANES, tm), lambda i: (0, i)),
                   pl.BlockSpec((tm, LANES), row),
                   pl.BlockSpec((N_EXPERTS, LANES), lambda i: (0, 0))],
        out_shape=[jax.ShapeDtypeStruct((t, d), F32),
                   jax.ShapeDtypeStruct((t, d), F32),
                   jax.ShapeDtypeStruct((SUBLANES, t), jnp.int32),
                   jax.ShapeDtypeStruct((t, LANES), F32),
                   jax.ShapeDtypeStruct((N_EXPERTS, LANES), F32)],
        scratch_shapes=[pltpu.VMEM((N_EXPERTS, LANES), F32)],
        compiler_params=_cparams("arbitrary"),
    )(y_ssd, y_att, x, mod, w_a, w_b, g, wr_hi, wr_lo, b_r)


def _plan_kernel(ri_ref, cnt_ref, dest_ref, blk_ref):
    tp = ri_ref.shape[1]
    nbl = blk_ref.shape[1]
    cnt = cnt_ref[...]
    padded = jnp.floor((cnt + (SLOT_BLOCK - 1)) * (1.0 / SLOT_BLOCK)) * SLOT_BLOCK
    r_ = lax.broadcasted_iota(jnp.int32, (N_EXPERTS, N_EXPERTS), 0)
    c_ = lax.broadcasted_iota(jnp.int32, (N_EXPERTS, N_EXPERTS), 1)
    pad_ends = _dot_f32((r_ >= c_).astype(F32), padded)
    pad_off = pad_ends - padded

    ri = ri_ref[...]
    erow = lax.broadcasted_iota(jnp.int32, (N_EXPERTS, tp), 0)
    pad_off_t = jnp.concatenate([pad_off] * (tp // LANES), axis=1)

    def dest(k):
        off = jnp.sum(jnp.where(erow == ri[k:k + 1, :], pad_off_t, 0.0), axis=0, keepdims=True)
        return off.astype(jnp.int32) + ri[TOP_K + k:TOP_K + k + 1, :]

    sub = lax.broadcasted_iota(jnp.int32, (SUBLANES, tp), 0)
    dest_ref[...] = jnp.where(sub == 0, dest(0), jnp.where(sub == 1, dest(1), 0))

    starts = (lax.broadcasted_iota(jnp.int32, (N_EXPERTS, nbl), 1) * SLOT_BLOCK).astype(F32)
    pad_ends_t = jnp.concatenate([pad_ends] * (nbl // LANES), axis=1)
    done = jnp.sum(jnp.where(pad_ends_t <= starts, 1, 0), axis=0, keepdims=True)
    blk_e = jnp.minimum(done, N_EXPERTS - 1)
    used = (pad_ends[N_EXPERTS - 1:N_EXPERTS, 0:1] * (1.0 / SLOT_BLOCK)).astype(jnp.int32)
    sub_b = lax.broadcasted_iota(jnp.int32, (SUBLANES, nbl), 0)
    blk_ref[...] = jnp.where(sub_b == 0, blk_e, jnp.where(sub_b == 1, used, 0))


def _slot_plan(route_i, counts, n_blocks, tp):
    t = route_i.shape[1]
    nbl = -(-n_blocks // LANES) * LANES
    return pl.pallas_call(
        _plan_kernel,
        grid=(t // tp,),
        in_specs=[pl.BlockSpec((SUBLANES, tp), lambda i: (0, i)),
                  pl.BlockSpec((N_EXPERTS, LANES), lambda i: (0, 0))],
        out_specs=[pl.BlockSpec((SUBLANES, tp), lambda i: (0, i)),
                   pl.BlockSpec((SUBLANES, nbl), lambda i: (0, 0))],
        out_shape=[jax.ShapeDtypeStruct((SUBLANES, t), jnp.int32),
                   jax.ShapeDtypeStruct((SUBLANES, nbl), jnp.int32)],
        compiler_params=_cparams("arbitrary"),
    )(route_i, counts)


def _dispatch_kernel(d0_ref, d1_ref, h2_ref, xb_in_ref, xb_ref, sem):
    del xb_in_ref
    tb = d0_ref.shape[0]

    def row_copy(r, d_ref):
        return pltpu.make_async_copy(h2_ref.at[pl.ds(r, 1)], xb_ref.at[pl.ds(d_ref[r], 1)], sem)

    for r in range(tb):
        row_copy(r, d0_ref).start()
        row_copy(r, d1_ref).start()

    def drain(k, carry):
        for u in range(DMA_UNROLL):
            row_copy(k * DMA_UNROLL + u, d0_ref).wait()
            row_copy(k * DMA_UNROLL + u, d1_ref).wait()
        return carry

    lax.fori_loop(0, tb // DMA_UNROLL, drain, 0)


def _dispatch(dest0, dest1, h2_rows, cap, tb):
    t = h2_rows.shape[0]
    xb0 = jnp.zeros((cap, D_MODEL), F32)
    smem = lambda: pl.BlockSpec((tb,), lambda i: (i,), memory_space=pltpu.SMEM)
    return pl.pallas_call(
        _dispatch_kernel,
        grid=(t // tb,),
        in_specs=[smem(), smem(),
                  pl.BlockSpec((tb, D_MODEL), lambda i: (i, 0)),
                  pl.BlockSpec(memory_space=pl.ANY)],
        out_specs=pl.BlockSpec(memory_space=pl.ANY),
        out_shape=jax.ShapeDtypeStruct((cap, D_MODEL), F32),
        scratch_shapes=[pltpu.SemaphoreType.DMA(())],
        input_output_aliases={3: 0},
        compiler_params=_cparams("arbitrary"),
    )(dest0, dest1, h2_rows, xb0)


def _experts_kernel(be_ref, used_ref, xb_ref, w1_ref, w3_ref, w2_ref, yb_ref, w1b, w3b, w2b):
    j = pl.program_id(0)
    prev = be_ref[jnp.maximum(j - 1, 0)]

    @pl.when((j == 0) | (be_ref[j] != prev))
    def _():
        w1b[...] = w1_ref[0].astype(BF16)
        w3b[...] = w3_ref[0].astype(BF16)
        w2b[...] = w2_ref[0].astype(BF16)

    @pl.when(j < used_ref[0])
    def _():
        xr = xb_ref[...].astype(BF16)
        hmid = (_silu(_dot(xr, w1b[...])) * _dot(xr, w3b[...])).astype(BF16)
        yb_ref[...] = _dot(hmid, w2b[...])

    @pl.when(j >= used_ref[0])
    def _():
        yb_ref[...] = jnp.zeros_like(yb_ref)


def _experts(block_e, used, xb, w1, w3, w2):
    cap = xb.shape[0]
    n_blocks = cap // SLOT_BLOCK
    d = w1.shape[1]
    xmap = lambda j, be, used: (jnp.minimum(j, jnp.maximum(used[0] - 1, 0)), 0)
    wmap = lambda j, be, used: (be[j], 0, 0)
    return pl.pallas_call(
        _experts_kernel,
        grid_spec=pltpu.PrefetchScalarGridSpec(
            num_scalar_prefetch=2,
            grid=(n_blocks,),
            in_specs=[pl.BlockSpec((SLOT_BLOCK, d), xmap),
                      pl.BlockSpec((1, d, D_EXPERT), wmap),
                      pl.BlockSpec((1, d, D_EXPERT), wmap),
                      pl.BlockSpec((1, D_EXPERT, d), wmap)],
            out_specs=pl.BlockSpec((SLOT_BLOCK, d), lambda j, be, used: (j, 0)),
            scratch_shapes=[pltpu.VMEM((d, D_EXPERT), BF16),
                            pltpu.VMEM((d, D_EXPERT), BF16),
                            pltpu.VMEM((D_EXPERT, d), BF16)]),
        out_shape=jax.ShapeDtypeStruct((cap, d), F32),
        compiler_params=_cparams("arbitrary"),
    )(block_e, used, xb, w1, w3, w2)


COMBINE_AHEAD = 2


def _combine_kernel(d0h_ref, d1h_ref, d0_ref, d1_ref, d0n_ref, d1n_ref, yb_ref, x1_ref, rf_ref, mod_ref, g_ref,
                    o_ref, ga_ref, gb_ref, sems):
    tm = x1_ref.shape[0]
    nbuf = COMBINE_AHEAD + 1
    i = pl.program_id(0)
    n = pl.num_programs(0)

    def row_copies(r, da, db, sl):
        return (pltpu.make_async_copy(yb_ref.at[pl.ds(da, 1)], ga_ref.at[sl, pl.ds(r, 1)], sems.at[sl]),
                pltpu.make_async_copy(yb_ref.at[pl.ds(db, 1)], gb_ref.at[sl, pl.ds(r, 1)], sems.at[sl]))

    def issue(da_ref, db_ref, sl):
        for r in range(tm):
            for cp in row_copies(r, da_ref[r], db_ref[r], sl):
                cp.start()

    def drain(da_ref, db_ref, sl):
        def body(k, carry):
            for u in range(DMA_UNROLL):
                r = k * DMA_UNROLL + u
                for cp in row_copies(r, da_ref[r], db_ref[r], sl):
                    cp.wait()
            return carry
        lax.fori_loop(0, tm // DMA_UNROLL, body, 0)

    @pl.when(i == 0)
    def _():
        for a in range(COMBINE_AHEAD):
            issue(d0h_ref.at[a, 0], d1h_ref.at[a, 0], a)

    slot = i % nbuf
    drain(d0_ref.at[0, 0], d1_ref.at[0, 0], slot)

    issue(d0n_ref.at[0, 0], d1n_ref.at[0, 0], (i + COMBINE_AHEAD) % nbuf)

    rf = rf_ref[...]
    y = rf[:, 0:1] * ga_ref[slot] + rf[:, 1:2] * gb_ref[slot]
    x2 = x1_ref[...] + mod_ref[0, 5:6, :] * y
    o_ref[0] = _rms(x2) * g_ref[...]

    @pl.when(i == n - 1)
    def _():
        for a in range(1, nbuf):
            drain(d0n_ref.at[0, 0], d1n_ref.at[0, 0], (i + a) % nbuf)


def _combine(dest0, dest1, yb, x1, route_f, mod, g, bsz, s, tm):
    t, d = x1.shape
    spb = s // tm
    nt = t // tm
    smem_head = lambda: pl.BlockSpec((COMBINE_AHEAD, 1, tm), lambda i: (0, 0, 0), memory_space=pltpu.SMEM)
    smem = lambda: pl.BlockSpec((1, 1, tm), lambda i: (i, 0, 0), memory_space=pltpu.SMEM)
    smem_next = lambda: pl.BlockSpec((1, 1, tm), lambda i: (jnp.minimum(i + COMBINE_AHEAD, nt - 1), 0, 0),
                                     memory_space=pltpu.SMEM)
    return pl.pallas_call(
        _combine_kernel,
        grid=(nt,),
        in_specs=[smem_head(), smem_head(), smem(), smem(), smem_next(), smem_next(),
                  pl.BlockSpec(memory_space=pl.ANY),
                  pl.BlockSpec((tm, d), lambda i: (i, 0)),
                  pl.BlockSpec((tm, LANES), lambda i: (i, 0)),
                  pl.BlockSpec((1, 6, d), lambda i: (i // spb, 0, 0)),
                  _const_spec((1, d))],
        out_specs=pl.BlockSpec((1, tm, d), lambda i: (i // spb, i % spb, 0)),
        out_shape=jax.ShapeDtypeStruct((bsz, s, d), F32),
        scratch_shapes=[pltpu.VMEM((COMBINE_AHEAD + 1, tm, d), F32),
                        pltpu.VMEM((COMBINE_AHEAD + 1, tm, d), F32),
                        pltpu.SemaphoreType.DMA((COMBINE_AHEAD + 1,))],
        compiler_params=_cparams("arbitrary"),
    )(*([dest0.reshape(nt, 1, tm), dest1.reshape(nt, 1, tm)] * 3), yb, x1, route_f, mod, g)


def _pad_lanes(v):
    return jnp.pad(v, ((0, 0), (0, LANES - v.shape[-1])))


def kernel(x, c, w_ada, b_ada, norm_mix_g, w_in, conv_w, conv_b, dt_bias, a_log, d_skip, ssd_norm_g, rel_bias,
           attn_norm_g, w_out, norm_ffn_g, w_router_group, b_router_group, w_router_expert, b_router_expert,
           w1, w3, w2, final_norm_g):
    bsz, s, d = x.shape
    t = bsz * s
    assert d == D_MODEL and w_ada.shape[0] == 1 and s % 512 == 0 and bsz % SSD_SEQS == 0 and t % 2048 == 0
    l = 0

    mod = _adaln_mod(c, w_ada[l], b_ada[l]).reshape(bsz, 6, d)

    o_z, o_xbc, o_dt, o_q, o_k = np.cumsum([D_SSD, CONV_DIM, SSD_HEADS, D_ATT, D_ATT]).tolist()
    wi = w_in[l]
    w_cat = jnp.concatenate([wi[:, :o_xbc], wi[:, o_dt:]], axis=1).astype(BF16)
    w_dt = _pad_lanes(wi[:, o_xbc:o_dt])
    z, xbc, q, k, v, dt_raw = _in_proj(x, mod, norm_mix_g[l][None], w_cat, w_dt, tm=512)

    y_ssd = _ssd(xbc, z, dt_raw, conv_w[l], conv_b[l][None], _pad_lanes(dt_bias[l][None]),
                 _pad_lanes(a_log[l][None]), jnp.repeat(d_skip[l], HEAD_DIM)[None], ssd_norm_g[l][None], bsz, s)
    y_att = _band_attn(q, k, v, _attn_bias_table(rel_bias[l]), attn_norm_g[l][None], bsz, s)

    wo = w_out[l].astype(BF16)
    tm_out = 256
    w_rt = jnp.concatenate([w_router_expert[l].T, w_router_group[l].T,
                            jnp.zeros((ROUTER_ROWS - N_EXPERTS - MOE_GROUPS, d), F32)], axis=0)
    wr_hi = w_rt.astype(BF16)
    wr_lo = (w_rt - wr_hi.astype(F32)).astype(BF16)
    b_rt = jnp.concatenate([b_router_expert[l], b_router_group[l],
                            jnp.zeros((ROUTER_ROWS - N_EXPERTS - MOE_GROUPS,), F32)])
    x1, h2_rows, route_i, route_f, counts = _out_router(
        y_ssd, y_att, x, mod, wo[:D_SSD], wo[D_SSD:], norm_ffn_g[l][None], wr_hi, wr_lo,
        jnp.broadcast_to(b_rt[:, None], (ROUTER_ROWS, tm_out)), tm=tm_out)

    n_blocks = t * TOP_K // SLOT_BLOCK + N_EXPERTS
    dest, blk = _slot_plan(route_i, counts, n_blocks, tp=2048)
    dest0, dest1 = dest[0], dest[1]
    block_e, used = blk[0, :n_blocks], blk[1, :1]

    xb = _dispatch(dest0, dest1, h2_rows, n_blocks * SLOT_BLOCK, tb=512)
    yb = _experts(block_e, used, xb, w1[l], w3[l], w2[l])
    return _combine(dest0, dest1, yb, x1, route_f, mod, final_norm_g[None], bsz, s, tm=256)
```

```python
import functools

import numpy as np
import jax
import jax.numpy as jnp
from jax import lax
from jax.experimental import pallas as pl
from jax.experimental.pallas import tpu as pltpu

D_MODEL = 1024
CHUNK = 64
HEAD_DIM = 64
D_MIX = 2 * D_MODEL
ATT_HEADS = D_MIX // 4 // HEAD_DIM
D_ATT = ATT_HEADS * HEAD_DIM
D_SSD = D_MIX - D_ATT
SSD_HEADS = D_SSD // HEAD_DIM
SSD_GROUPS = 4
HEADS_PER_GROUP = SSD_HEADS // SSD_GROUPS
SSD_STATE = 128
CONV_WIDTH = 4
D_BC = SSD_GROUPS * SSD_STATE
CONV_DIM = D_SSD + 2 * D_BC
N_PREV_CHUNKS = 8
REL_CLIP = 256
MOE_GROUPS = 4
EXPERTS_PER_GROUP = 8
N_EXPERTS = MOE_GROUPS * EXPERTS_PER_GROUP
TOP_K = 2
D_EXPERT = 512
SLOT_BLOCK = 256
EPS = 1e-6

LANES = 128
SUBLANES = 8
DMA_UNROLL = 8
LOG2E = 1.4426950408889634
NEG = -1e30
VMEM_LIMIT = 56 * 1024 * 1024

F32 = jnp.float32
BF16 = jnp.bfloat16
HIGHEST = lax.Precision.HIGHEST


def _cparams(*sem):
    return pltpu.CompilerParams(dimension_semantics=sem, vmem_limit_bytes=VMEM_LIMIT)


def _const_spec(shape):
    nd = len(shape)
    return pl.BlockSpec(shape, lambda *_: (0,) * nd, pipeline_mode=pl.Buffered(1))


def _sigmoid(u):
    return 1.0 / (1.0 + jnp.exp(-u))


def _silu(u):
    return u * _sigmoid(u)


def _rms(xf):
    return xf * lax.rsqrt(jnp.mean(xf * xf, axis=-1, keepdims=True) + EPS)


def _bf16_head(x):
    bits = lax.bitcast_convert_type(x, jnp.uint32) & jnp.uint32(0xFFFF0000)
    return lax.bitcast_convert_type(bits, F32)


def _dot(a, b):
    return jnp.dot(a, b, preferred_element_type=F32)


def _dot_nt(a, b):
    return lax.dot_general(a, b, (((1,), (1,)), ((), ())), preferred_element_type=F32)


def _dot_f32(a, b):
    return jnp.dot(a, b, preferred_element_type=F32, precision=HIGHEST)


def _adaln_kernel(c_ref, w_ref, b_ref, o_ref):
    o_ref[...] = _dot_f32(_silu(c_ref[...]), w_ref[...]) + b_ref[...]


def _adaln_mod(c, w, b):
    bsz, d = c.shape
    n = w.shape[1]
    tn = 1536
    return pl.pallas_call(
        _adaln_kernel,
        grid=(n // tn,),
        in_specs=[pl.BlockSpec((bsz, d), lambda j: (0, 0)),
                  pl.BlockSpec((d, tn), lambda j: (0, j)),
                  pl.BlockSpec((1, tn), lambda j: (0, j))],
        out_specs=pl.BlockSpec((bsz, tn), lambda j: (0, j)),
        out_shape=jax.ShapeDtypeStruct((bsz, n), F32),
        compiler_params=_cparams("arbitrary"),
    )(c, w, b.reshape(1, n))


IN_SEGS = (("z", D_SSD), ("xbc", CONV_DIM), ("q", D_ATT), ("k", D_ATT), ("v", D_ATT))
IN_WIDTH = sum(n for _, n in IN_SEGS)


def _in_proj_kernel(x_ref, mod_ref, g_ref, w_ref, wdh_ref, wdl_ref, z_ref, xbc_ref, q_ref, k_ref, v_ref, dt_ref):
    xf = x_ref[0]
    h = _rms(xf) * g_ref[...] * (1.0 + mod_ref[0, 1:2, :]) + mod_ref[0, 0:1, :]
    hb = h.astype(BF16)
    off = 0
    for (name, n), o_ref in zip(IN_SEGS, (z_ref, xbc_ref, q_ref, k_ref, v_ref)):
        acc = _dot(hb, w_ref[:, off:off + n])
        if name == "q":
            acc = acc * (HEAD_DIM ** -0.5 * LOG2E)
        o_ref[...] = acc.astype(o_ref.dtype)
        off += n
    h_head = _bf16_head(h)
    h_hi = h_head.astype(BF16)
    h_lo = (h - h_head).astype(BF16)
    dt_ref[...] = (_dot(h_hi, wdh_ref[...]) + _dot(h_lo, wdh_ref[...])) + _dot(h_hi, wdl_ref[...])


def _in_proj(x, mod, g, w_cat, w_dt_hi, w_dt_lo, tm):
    bsz, s, d = x.shape
    t = bsz * s
    spb = s // tm
    row = lambda i: (i, 0)
    outs = [jax.ShapeDtypeStruct((t, n), BF16) for _, n in IN_SEGS] + [jax.ShapeDtypeStruct((t, LANES), F32)]
    out_specs = [pl.BlockSpec((tm, n), row) for _, n in IN_SEGS] + [pl.BlockSpec((tm, LANES), row)]
    return pl.pallas_call(
        _in_proj_kernel,
        grid=(t // tm,),
        in_specs=[pl.BlockSpec((1, tm, d), lambda i: (i // spb, i % spb, 0)),
                  pl.BlockSpec((1, 6, d), lambda i: (i // spb, 0, 0)),
                  _const_spec((1, d)),
                  _const_spec((d, IN_WIDTH)),
                  _const_spec((d, LANES)),
                  _const_spec((d, LANES))],
        out_specs=out_specs,
        out_shape=outs,
        compiler_params=_cparams("arbitrary"),
    )(x, mod, g, w_cat, w_dt_hi, w_dt_lo)


CONV_TAIL = CHUNK
SSD_SEQS = 2


def _conv_shift_matrix():
    sel = np.zeros((CONV_WIDTH * CHUNK, CONV_TAIL + CHUNK), np.float32)
    for w in range(CONV_WIDTH):
        for t in range(CHUNK):
            sel[w * CHUNK + t, CONV_TAIL + t - (CONV_WIDTH - 1) + w] = 1.0
    return jnp.asarray(sel, BF16)


def _head_expand_matrix():
    e = np.zeros((LANES, D_SSD), np.float32)
    for h in range(SSD_HEADS):
        e[h, h * HEAD_DIM:(h + 1) * HEAD_DIM] = 1.0
    return jnp.asarray(e, BF16)


def _ssd_kernel(xbc_ref, z_ref, dt_ref, shift_ref, expand_ref, cw_ref, cb_ref, dtb_ref, alog_ref, dskip_ref, ng_ref,
                o_ref, *carry_refs):
    L = CHUNK
    P = HEAD_DIM
    GW = HEADS_PER_GROUP * P
    assert L == P
    seqs = range(SSD_SEQS)
    tail_refs, state_refs = carry_refs[:SSD_SEQS], carry_refs[SSD_SEQS:]
    j = pl.program_id(1)

    @pl.when(j == 0)
    def _():
        for tail_ref, state_ref in zip(tail_refs, state_refs):
            tail_ref[0] = jnp.zeros((CONV_TAIL, CONV_DIM), BF16)
            state_ref[...] = jnp.zeros_like(state_ref)

    dtv = [dt_ref[b] + dtb_ref[...] for b in seqs]
    dt = [jnp.maximum(v, 0.0) + jnp.log1p(jnp.exp(-jnp.abs(v))) for v in dtv]
    neg_a = -jnp.exp(alog_ref[...])
    ri = lax.broadcasted_iota(jnp.int32, (L, L), 0)
    ci = lax.broadcasted_iota(jnp.int32, (L, L), 1)
    tri = (ri >= ci).astype(F32)
    acum = [_dot_f32(tri, d * neg_a) for d in dt]

    shifted = [_dot(shift_ref[...], jnp.concatenate([tail_refs[b][j % 2], xbc_ref[b]], axis=0))
               for b in seqs]
    for b in seqs:
        tail_refs[b][(j + 1) % 2] = xbc_ref[b, L - CONV_TAIL:L, :]
    xc = []
    for b in seqs:
        u = cb_ref[...]
        for w in range(CONV_WIDTH):
            u = u + shifted[b][w * L:(w + 1) * L, :] * cw_ref[w:w + 1, :]
        xc.append(_silu(u))
    xs = [m[:, :D_SSD] for m in xc]
    bm = [[xc[b][:, D_SSD + g * SSD_STATE:D_SSD + (g + 1) * SSD_STATE] for g in range(SSD_GROUPS)] for b in seqs]
    cm_b = [[xc[b][:, D_SSD + D_BC + g * SSD_STATE:D_SSD + D_BC + (g + 1) * SSD_STATE].astype(BF16)
             for g in range(SSD_GROUPS)] for b in seqs]
    cb = [[_dot_nt(cm_b[b][g], bm[b][g].astype(BF16)) for g in range(SSD_GROUPS)] for b in seqs]

    def bf16_terms(m):
        hi = _bf16_head(m)
        r1 = m - hi
        mid = _bf16_head(r1)
        return [hi.astype(BF16), mid.astype(BF16), (r1 - mid).astype(BF16)]

    ex = [_dot(jnp.concatenate(bf16_terms(dt[b]) + bf16_terms(acum[b]), axis=0), expand_ref[...]) for b in seqs]
    dt_x = [(e[0:L] + e[L:2 * L]) + e[2 * L:3 * L] for e in ex]
    acum_x = [(e[3 * L:4 * L] + e[4 * L:5 * L]) + e[5 * L:6 * L] for e in ex]
    acum_t = [m.T for m in acum]
    acum_keys = [jnp.concatenate([t[h:h + 1, :] for h in range(SSD_HEADS)], axis=-1) for t in acum_t]
    li = lax.broadcasted_iota(jnp.int32, (L, D_SSD), 0)
    si = lax.broadcasted_iota(jnp.int32, (L, D_SSD), 1) & (P - 1)
    decay_ls = [jnp.exp(jnp.where(li >= si, acum_x[b] - acum_keys[b], NEG)) for b in seqs]
    a_last = [m[L - 1:L, :] for m in acum_x]
    exp_a = [jnp.exp(m) for m in acum_x]
    chunk_dec = [jnp.exp(m) for m in a_last]
    xdt = [xs[b] * dt_x[b] for b in seqs]
    xdt_b = [m.astype(BF16) for m in xdt]
    xdec_b = [(xdt[b] * jnp.exp(a_last[b] - acum_x[b])).astype(BF16) for b in seqs]
    lane2 = lax.broadcasted_iota(jnp.int32, (L, 2 * P), 1)
    first_b = (lane2 < P).astype(F32).astype(BF16)
    second_b = (lane2 >= P).astype(F32).astype(BF16)

    ys = [[] for _ in seqs]
    for g in range(SSD_GROUPS):
        gl = slice(g * GW, (g + 1) * GW)
        prev = [state_refs[b][g] for b in seqs]
        y_off = [_dot(cm_b[b][g], prev[b].astype(BF16)) * exp_a[b][:, gl] for b in seqs]
        new_state = [prev[b] * chunk_dec[b][:, gl] + _dot(bm[b][g].T.astype(BF16), xdec_b[b][:, gl]) for b in seqs]
        for b in seqs:
            state_refs[b][g] = new_state[b]
        for b in seqs:
            cb2 = jnp.concatenate([cb[b][g], cb[b][g]], axis=-1)
            yg = []
            for j in range(GW // (2 * P)):
                pl_ = slice(g * GW + j * 2 * P, g * GW + (j + 1) * 2 * P)
                m2 = (cb2 * decay_ls[b][:, pl_]).astype(BF16)
                x2 = xdt_b[b][:, pl_]
                yg.append(_dot(m2, jnp.concatenate([x2 * first_b, x2 * second_b], axis=0)))
            ys[b].append(jnp.concatenate(yg, axis=-1) + y_off[b])

    y = [jnp.concatenate(ys[b], axis=-1) + xs[b] * dskip_ref[...] for b in seqs]
    gated = [y[b] * _silu(z_ref[b].astype(F32)) for b in seqs]
    for b in seqs:
        o_ref[b] = (_rms(gated[b]) * ng_ref[...]).astype(o_ref.dtype)


def _ssd(xbc, z, dt_raw, conv_w, conv_b, dt_bias, a_log, d_skip, norm_g, bsz, s):
    nc = s // CHUNK
    seq = lambda width: pl.BlockSpec((SSD_SEQS, CHUNK, width), lambda b, j: (b, j, 0))
    return pl.pallas_call(
        _ssd_kernel,
        grid=(bsz // SSD_SEQS, nc),
        in_specs=[seq(CONV_DIM), seq(D_SSD), seq(LANES),
                  _const_spec((CONV_WIDTH * CHUNK, CONV_TAIL + CHUNK)),
                  _const_spec((LANES, D_SSD)),
                  _const_spec((CONV_WIDTH, CONV_DIM)),
                  _const_spec((1, CONV_DIM)),
                  _const_spec((1, LANES)),
                  _const_spec((1, LANES)),
                  _const_spec((1, D_SSD)),
                  _const_spec((1, D_SSD))],
        out_specs=seq(D_SSD),
        out_shape=jax.ShapeDtypeStruct((bsz, s, D_SSD), BF16),
        scratch_shapes=[pltpu.VMEM((2, CONV_TAIL, CONV_DIM), BF16)] * SSD_SEQS
                       + [pltpu.VMEM((SSD_GROUPS, SSD_STATE, HEADS_PER_GROUP * HEAD_DIM), F32)] * SSD_SEQS,
        compiler_params=_cparams("arbitrary", "arbitrary"),
    )(xbc.reshape(bsz, s, CONV_DIM), z.reshape(bsz, s, D_SSD), dt_raw.reshape(bsz, s, LANES),
      _conv_shift_matrix(), _head_expand_matrix(), conv_w, conv_b, dt_bias, a_log, d_skip, norm_g).reshape(bsz * s, D_SSD)


ATT_QB = 256
ATT_KB = ATT_QB + N_PREV_CHUNKS * CHUNK


def _attn_kernel(q_ref, k2_ref, k1_ref, k0_ref, v2_ref, v1_ref, v0_ref, bias_ref, g_ref, o_ref):
    i = pl.program_id(1)
    k_refs = (k2_ref, k1_ref, k0_ref)
    v_refs = (v2_ref, v1_ref, v0_ref)
    nkb_max = ATT_KB // ATT_QB

    def attend(nkb):
        lo = (nkb_max - nkb) * ATT_QB
        k_all = jnp.concatenate([r[...] for r in k_refs[nkb_max - nkb:]], axis=0)
        v_all = jnp.concatenate([r[...] for r in v_refs[nkb_max - nkb:]], axis=0)
        outs = []
        for h in range(ATT_HEADS):
            sl = slice(h * HEAD_DIM, (h + 1) * HEAD_DIM)
            sc = _dot_nt(q_ref[:, sl], k_all[:, sl]) + bias_ref[h, :, lo:]
            p = jnp.exp2(sc - jnp.max(sc, axis=-1, keepdims=True))
            o = _dot(p.astype(BF16), v_all[:, sl])
            outs.append(o / jnp.sum(p, axis=-1, keepdims=True))
        y = jnp.concatenate(outs, axis=-1)
        o_ref[...] = (_rms(y) * g_ref[...]).astype(o_ref.dtype)

    for nkb in range(1, nkb_max):
        pl.when(i == nkb - 1)(functools.partial(attend, nkb))
    pl.when(i >= nkb_max - 1)(functools.partial(attend, nkb_max))


def _attn_bias_table(rel_bias):
    nh = rel_bias.shape[0]
    period = ATT_QB + ATT_KB
    back = ATT_KB - ATT_QB
    n_clip_far = back - REL_CLIP + ATT_QB - 1
    assert n_clip_far + 2 * REL_CLIP + 1 == period
    by_offset = jnp.concatenate([jnp.broadcast_to(rel_bias[:, 2 * REL_CLIP:], (nh, n_clip_far)),
                                 rel_bias[:, ::-1]], axis=1).astype(F32)
    cyc = jnp.roll(by_offset, -(ATT_QB - 1), axis=1)
    skew = jnp.tile(cyc, (1, ATT_QB))[:, :ATT_QB * (period - 1)].reshape(nh, ATT_QB, period - 1)
    qi = np.arange(ATT_QB)[:, None]
    kj = np.arange(ATT_KB)[None, :]
    dchunk = back // CHUNK + qi // CHUNK - kj // CHUNK
    in_band = (dchunk >= 0) & (dchunk <= N_PREV_CHUNKS)
    return jnp.where(in_band[None], skew[:, :, :ATT_KB] * LOG2E, NEG)


def _band_attn(q, k, v, bias, norm_g, bsz, s):
    t = bsz * s
    nq = s // ATT_QB
    qmap = lambda b, i: (b * nq + i, 0)

    def kmap(back):
        return lambda b, i: (b * nq + jnp.maximum(i - back, 0), 0)

    kv_specs = [pl.BlockSpec((ATT_QB, D_ATT), kmap(back)) for back in (2, 1, 0)]
    return pl.pallas_call(
        _attn_kernel,
        grid=(bsz, nq),
        in_specs=[pl.BlockSpec((ATT_QB, D_ATT), qmap)] + kv_specs + kv_specs
                 + [_const_spec((ATT_HEADS, ATT_QB, ATT_KB)), _const_spec((1, D_ATT))],
        out_specs=pl.BlockSpec((ATT_QB, D_ATT), qmap),
        out_shape=jax.ShapeDtypeStruct((t, D_ATT), BF16),
        compiler_params=_cparams("arbitrary", "arbitrary"),
    )(q, k, k, k, v, v, v, bias, norm_g)


ROUTER_ROWS = 48


def _out_router_kernel(ys_ref, ya_ref, x_ref, mod_ref, wa_ref, wb_ref, g_ref, wrh_ref, wrl_ref, br_ref,
                       x1_ref, h2_ref, ri_ref, rf_ref, cnt_ref, run_ref):
    tm = x_ref.shape[1]

    @pl.when(pl.program_id(0) == 0)
    def _():
        run_ref[...] = jnp.zeros_like(run_ref)

    y = _dot(ys_ref[...], wa_ref[...]) + _dot(ya_ref[...], wb_ref[...])
    x1 = x_ref[0] + mod_ref[0, 2:3, :] * y
    x1_ref[...] = x1
    h2 = _rms(x1) * g_ref[...] * (1.0 + mod_ref[0, 4:5, :]) + mod_ref[0, 3:4, :]
    h2_ref[...] = h2

    h_head = _bf16_head(h2)
    h_hi = h_head.astype(BF16)
    h_lo = (h2 - h_head).astype(BF16)
    lt = (_dot_nt(wrh_ref[...], h_hi) + _dot_nt(wrh_ref[...], h_lo) + _dot_nt(wrl_ref[...], h_hi)) + br_ref[...]

    sub = lax.broadcasted_iota(jnp.int32, (SUBLANES, tm), 0)
    assert EXPERTS_PER_GROUP == SUBLANES and MOE_GROUPS <= SUBLANES

    def first_argmax(vals, mx):
        return jnp.min(jnp.where(vals == mx, sub, SUBLANES), axis=0, keepdims=True)

    gl = jnp.where(sub < MOE_GROUPS, lt[N_EXPERTS:N_EXPERTS + SUBLANES], NEG)
    gexp = jnp.exp(gl - jnp.max(gl, axis=0, keepdims=True))
    gprob = gexp / jnp.sum(gexp, axis=0, keepdims=True)
    g_p = jnp.max(gprob, axis=0, keepdims=True)
    g_idx = first_argmax(gprob, g_p)

    el = lt[(MOE_GROUPS - 1) * SUBLANES:MOE_GROUPS * SUBLANES]
    for g in range(MOE_GROUPS - 2, -1, -1):
        el = jnp.where(g_idx == g, lt[g * SUBLANES:(g + 1) * SUBLANES], el)
    eexp = jnp.exp(el - jnp.max(el, axis=0, keepdims=True))
    eprob = eexp / jnp.sum(eexp, axis=0, keepdims=True)
    p1 = jnp.max(eprob, axis=0, keepdims=True)
    i1 = first_argmax(eprob, p1)
    eprob2 = jnp.where(sub == i1, -1.0, eprob)
    p2 = jnp.max(eprob2, axis=0, keepdims=True)
    i2 = first_argmax(eprob2, p2)
    psum = p1 + p2
    gate1 = g_p * (p1 / psum)
    gate2 = g_p * (p2 / psum)
    e1 = g_idx * EXPERTS_PER_GROUP + i1
    e2 = g_idx * EXPERTS_PER_GROUP + i2

    erow = lax.broadcasted_iota(jnp.int32, (N_EXPERTS, tm), 0)
    oh1 = (erow == e1).astype(F32)
    oh2 = (erow == e2).astype(F32)
    both = oh1 + oh2
    ri_ = lax.broadcasted_iota(jnp.int32, (tm, tm), 0)
    ci_ = lax.broadcasted_iota(jnp.int32, (tm, tm), 1)
    run = run_ref[...]
    before = _dot(both.astype(BF16), (ri_ < ci_).astype(BF16)) + jnp.concatenate([run] * (tm // LANES), axis=1)
    rank1 = jnp.sum(oh1 * before, axis=0, keepdims=True).astype(jnp.int32)
    rank2 = jnp.sum(oh2 * before, axis=0, keepdims=True).astype(jnp.int32)
    run = run + jnp.sum(both, axis=1, keepdims=True)
    run_ref[...] = run
    cnt_ref[...] = run

    ri_ref[...] = jnp.where(sub == 0, e1, jnp.where(sub == 1, e2, jnp.where(sub == 2, rank1, jnp.where(
        sub == 3, rank2, 0))))
    gates = jnp.where(sub == 0, gate1, jnp.where(sub == 1, gate2, 0.0))
    rf_ref[...] = jnp.concatenate([gates, jnp.zeros((LANES - SUBLANES, tm), F32)], axis=0).T


def _out_router(y_ssd, y_att, x, mod, w_a, w_b, g, wr_hi, wr_lo, b_r, tm):
    bsz, s, d = x.shape
    t = bsz * s
    spb = s // tm
    row = lambda i: (i, 0)
    return pl.pallas_call(
        _out_router_kernel,
        grid=(t // tm,),
        in_specs=[pl.BlockSpec((tm, D_SSD), row),
                  pl.BlockSpec((tm, D_ATT), row),
                  pl.BlockSpec((1, tm, d), lambda i: (i // spb, i % spb, 0)),
                  pl.BlockSpec((1, 6, d), lambda i: (i // spb, 0, 0)),
                  _const_spec((D_SSD, d)),
                  _const_spec((D_ATT, d)),
                  _const_spec((1, d)),
                  _const_spec((ROUTER_ROWS, d)),
                  _const_spec((ROUTER_ROWS, d)),
                  _const_spec((ROUTER_ROWS, tm))],
        out_specs=[pl.BlockSpec((tm, d), row),
                   pl.BlockSpec((tm, d), row),
                   pl.BlockSpec((SUBLANES, tm), lambda i: (0, i)),
                   pl.BlockSpec((tm, LANES), row),
                   pl.BlockSpec((N_EXPERTS, LANES), lambda i: (0, 0))],
        out_shape=[jax.ShapeDtypeStruct((t, d), F32),
                   jax.ShapeDtypeStruct((t, d), F32),
                   jax.ShapeDtypeStruct((SUBLANES, t), jnp.int32),
                   jax.ShapeDtypeStruct((t, LANES), F32),
                   jax.ShapeDtypeStruct((N_EXPERTS, LANES), F32)],
        scratch_shapes=[pltpu.VMEM((N_EXPERTS, LANES), F32)],
        compiler_params=_cparams("arbitrary"),
    )(y_ssd, y_att, x, mod, w_a, w_b, g, wr_hi, wr_lo, b_r)


def _plan_kernel(ri_ref, cnt_ref, dest_ref, blk_ref):
    tp = ri_ref.shape[1]
    nbl = blk_ref.shape[1]
    cnt = cnt_ref[...]
    padded = jnp.floor((cnt + (SLOT_BLOCK - 1)) * (1.0 / SLOT_BLOCK)) * SLOT_BLOCK
    r_ = lax.broadcasted_iota(jnp.int32, (N_EXPERTS, N_EXPERTS), 0)
    c_ = lax.broadcasted_iota(jnp.int32, (N_EXPERTS, N_EXPERTS), 1)
    pad_ends = _dot_f32((r_ >= c_).astype(F32), padded)
    pad_off = pad_ends - padded

    ri = ri_ref[...]
    erow = lax.broadcasted_iota(jnp.int32, (N_EXPERTS, tp), 0)
    pad_off_t = jnp.concatenate([pad_off] * (tp // LANES), axis=1)

    def dest(k):
        off = jnp.sum(jnp.where(erow == ri[k:k + 1, :], pad_off_t, 0.0), axis=0, keepdims=True)
        return off.astype(jnp.int32) + ri[TOP_K + k:TOP_K + k + 1, :]

    sub = lax.broadcasted_iota(jnp.int32, (SUBLANES, tp), 0)
    dest_ref[...] = jnp.where(sub == 0, dest(0), jnp.where(sub == 1, dest(1), 0))

    starts = (lax.broadcasted_iota(jnp.int32, (N_EXPERTS, nbl), 1) * SLOT_BLOCK).astype(F32)
    pad_ends_t = jnp.concatenate([pad_ends] * (nbl // LANES), axis=1)
    done = jnp.sum(jnp.where(pad_ends_t <= starts, 1, 0), axis=0, keepdims=True)
    blk_e = jnp.minimum(done, N_EXPERTS - 1)
    used = (pad_ends[N_EXPERTS - 1:N_EXPERTS, 0:1] * (1.0 / SLOT_BLOCK)).astype(jnp.int32)
    sub_b = lax.broadcasted_iota(jnp.int32, (SUBLANES, nbl), 0)
    blk_ref[...] = jnp.where(sub_b == 0, blk_e, jnp.where(sub_b == 1, used, 0))


def _slot_plan(route_i, counts, n_blocks, tp):
    t = route_i.shape[1]
    nbl = -(-n_blocks // LANES) * LANES
    return pl.pallas_call(
        _plan_kernel,
        grid=(t // tp,),
        in_specs=[pl.BlockSpec((SUBLANES, tp), lambda i: (0, i)),
                  pl.BlockSpec((N_EXPERTS, LANES), lambda i: (0, 0))],
        out_specs=[pl.BlockSpec((SUBLANES, tp), lambda i: (0, i)),
                   pl.BlockSpec((SUBLANES, nbl), lambda i: (0, 0))],
        out_shape=[jax.ShapeDtypeStruct((SUBLANES, t), jnp.int32),
                   jax.ShapeDtypeStruct((SUBLANES, nbl), jnp.int32)],
        compiler_params=_cparams("arbitrary"),
    )(route_i, counts)


def _dispatch_kernel(d0_ref, d1_ref, h2_ref, xb_in_ref, xb_ref, sem):
    del xb_in_ref
    tb = d0_ref.shape[0]

    def row_copy(r, d_ref):
        return pltpu.make_async_copy(h2_ref.at[pl.ds(r, 1)], xb_ref.at[pl.ds(d_ref[r], 1)], sem)

    for r in range(tb):
        row_copy(r, d0_ref).start()
        row_copy(r, d1_ref).start()

    def drain(k, carry):
        for u in range(DMA_UNROLL):
            row_copy(k * DMA_UNROLL + u, d0_ref).wait()
            row_copy(k * DMA_UNROLL + u, d1_ref).wait()
        return carry

    lax.fori_loop(0, tb // DMA_UNROLL, drain, 0)


def _dispatch(dest0, dest1, h2_rows, cap, tb):
    t = h2_rows.shape[0]
    xb0 = jnp.zeros((cap, D_MODEL), F32)
    smem = lambda: pl.BlockSpec((tb,), lambda i: (i,), memory_space=pltpu.SMEM)
    return pl.pallas_call(
        _dispatch_kernel,
        grid=(t // tb,),
        in_specs=[smem(), smem(),
                  pl.BlockSpec((tb, D_MODEL), lambda i: (i, 0)),
                  pl.BlockSpec(memory_space=pl.ANY)],
        out_specs=pl.BlockSpec(memory_space=pl.ANY),
        out_shape=jax.ShapeDtypeStruct((cap, D_MODEL), F32),
        scratch_shapes=[pltpu.SemaphoreType.DMA(())],
        input_output_aliases={3: 0},
        compiler_params=_cparams("arbitrary"),
    )(dest0, dest1, h2_rows, xb0)


def _experts_kernel(be_ref, used_ref, xb_ref, w1_ref, w3_ref, w2_ref, yb_ref, w1b, w3b, w2b):
    j = pl.program_id(0)
    prev = be_ref[jnp.maximum(j - 1, 0)]

    @pl.when((j == 0) | (be_ref[j] != prev))
    def _():
        w1b[...] = w1_ref[0].astype(BF16)
        w3b[...] = w3_ref[0].astype(BF16)
        w2b[...] = w2_ref[0].astype(BF16)

    @pl.when(j < used_ref[0])
    def _():
        xr = xb_ref[...].astype(BF16)
        hmid = (_silu(_dot(xr, w1b[...])) * _dot(xr, w3b[...])).astype(BF16)
        yb_ref[...] = _dot(hmid, w2b[...])

    @pl.when(j >= used_ref[0])
    def _():
        yb_ref[...] = jnp.zeros_like(yb_ref)


def _experts(block_e, used, xb, w1, w3, w2):
    cap = xb.shape[0]
    n_blocks = cap // SLOT_BLOCK
    d = w1.shape[1]
    xmap = lambda j, be, used: (jnp.minimum(j, jnp.maximum(used[0] - 1, 0)), 0)
    wmap = lambda j, be, used: (be[j], 0, 0)
    return pl.pallas_call(
        _experts_kernel,
        grid_spec=pltpu.PrefetchScalarGridSpec(
            num_scalar_prefetch=2,
            grid=(n_blocks,),
            in_specs=[pl.BlockSpec((SLOT_BLOCK, d), xmap),
                      pl.BlockSpec((1, d, D_EXPERT), wmap),
                      pl.BlockSpec((1, d, D_EXPERT), wmap),
                      pl.BlockSpec((1, D_EXPERT, d), wmap)],
            out_specs=pl.BlockSpec((SLOT_BLOCK, d), lambda j, be, used: (j, 0)),
            scratch_shapes=[pltpu.VMEM((d, D_EXPERT), BF16),
                            pltpu.VMEM((d, D_EXPERT), BF16),
                            pltpu.VMEM((D_EXPERT, d), BF16)]),
        out_shape=jax.ShapeDtypeStruct((cap, d), F32),
        compiler_params=_cparams("arbitrary"),
    )(block_e, used, xb, w1, w3, w2)


COMBINE_AHEAD = 2


def _combine_kernel(d0h_ref, d1h_ref, d0_ref, d1_ref, d0n_ref, d1n_ref, yb_ref, x1_ref, rf_ref, mod_ref, g_ref,
                    o_ref, ga_ref, gb_ref, sems):
    tm = x1_ref.shape[0]
    nbuf = COMBINE_AHEAD + 1
    i = pl.program_id(0)
    n = pl.num_programs(0)

    def row_copies(r, da, db, sl):
        return (pltpu.make_async_copy(yb_ref.at[pl.ds(da, 1)], ga_ref.at[sl, pl.ds(r, 1)], sems.at[sl]),
                pltpu.make_async_copy(yb_ref.at[pl.ds(db, 1)], gb_ref.at[sl, pl.ds(r, 1)], sems.at[sl]))

    def issue(da_ref, db_ref, sl):
        for r in range(tm):
            for cp in row_copies(r, da_ref[r], db_ref[r], sl):
                cp.start()

    def drain(da_ref, db_ref, sl):
        def body(k, carry):
            for u in range(DMA_UNROLL):
                r = k * DMA_UNROLL + u
                for cp in row_copies(r, da_ref[r], db_ref[r], sl):
                    cp.wait()
            return carry
        lax.fori_loop(0, tm // DMA_UNROLL, body, 0)

    @pl.when(i == 0)
    def _():
        for a in range(COMBINE_AHEAD):
            issue(d0h_ref.at[a, 0], d1h_ref.at[a, 0], a)

    slot = i % nbuf
    drain(d0_ref.at[0, 0], d1_ref.at[0, 0], slot)

    issue(d0n_ref.at[0, 0], d1n_ref.at[0, 0], (i + COMBINE_AHEAD) % nbuf)

    rf = rf_ref[...]
    y = rf[:, 0:1] * ga_ref[slot] + rf[:, 1:2] * gb_ref[slot]
    x2 = x1_ref[...] + mod_ref[0, 5:6, :] * y
    o_ref[0] = _rms(x2) * g_ref[...]

    @pl.when(i == n - 1)
    def _():
        for a in range(1, nbuf):
            drain(d0n_ref.at[0, 0], d1n_ref.at[0, 0], (i + a) % nbuf)


def _combine(dest0, dest1, yb, x1, route_f, mod, g, bsz, s, tm):
    t, d = x1.shape
    spb = s // tm
    nt = t // tm
    smem_head = lambda: pl.BlockSpec((COMBINE_AHEAD, 1, tm), lambda i: (0, 0, 0), memory_space=pltpu.SMEM)
    smem = lambda: pl.BlockSpec((1, 1, tm), lambda i: (i, 0, 0), memory_space=pltpu.SMEM)
    smem_next = lambda: pl.BlockSpec((1, 1, tm), lambda i: (jnp.minimum(i + COMBINE_AHEAD, nt - 1), 0, 0),
                                     memory_space=pltpu.SMEM)
    return pl.pallas_call(
        _combine_kernel,
        grid=(nt,),
        in_specs=[smem_head(), smem_head(), smem(), smem(), smem_next(), smem_next(),
                  pl.BlockSpec(memory_space=pl.ANY),
                  pl.BlockSpec((tm, d), lambda i: (i, 0)),
                  pl.BlockSpec((tm, LANES), lambda i: (i, 0)),
                  pl.BlockSpec((1, 6, d), lambda i: (i // spb, 0, 0)),
                  _const_spec((1, d))],
        out_specs=pl.BlockSpec((1, tm, d), lambda i: (i // spb, i % spb, 0)),
        out_shape=jax.ShapeDtypeStruct((bsz, s, d), F32),
        scratch_shapes=[pltpu.VMEM((COMBINE_AHEAD + 1, tm, d), F32),
                        pltpu.VMEM((COMBINE_AHEAD + 1, tm, d), F32),
                        pltpu.SemaphoreType.DMA((COMBINE_AHEAD + 1,))],
        compiler_params=_cparams("arbitrary"),
    )(*([dest0.reshape(nt, 1, tm), dest1.reshape(nt, 1, tm)] * 3), yb, x1, route_f, mod, g)


def _pad_lanes(v):
    return jnp.pad(v, ((0, 0), (0, LANES - v.shape[-1])))


def kernel(x, c, w_ada, b_ada, norm_mix_g, w_in, conv_w, conv_b, dt_bias, a_log, d_skip, ssd_norm_g, rel_bias,
           attn_norm_g, w_out, norm_ffn_g, w_router_group, b_router_group, w_router_expert, b_router_expert,
           w1, w3, w2, final_norm_g):
    bsz, s, d = x.shape
    t = bsz * s
    assert d == D_MODEL and w_ada.shape[0] == 1 and s % 512 == 0 and bsz % SSD_SEQS == 0 and t % 2048 == 0
    l = 0

    mod = _adaln_mod(c, w_ada[l], b_ada[l]).reshape(bsz, 6, d)

    o_z, o_xbc, o_dt, o_q, o_k = np.cumsum([D_SSD, CONV_DIM, SSD_HEADS, D_ATT, D_ATT]).tolist()
    wi = w_in[l]
    w_cat = jnp.concatenate([wi[:, :o_xbc], wi[:, o_dt:]], axis=1).astype(BF16)
    w_dt = _pad_lanes(wi[:, o_xbc:o_dt])
    w_dt_hi = w_dt.astype(BF16)
    w_dt_lo = (w_dt - w_dt_hi.astype(F32)).astype(BF16)
    z, xbc, q, k, v, dt_raw = _in_proj(x, mod, norm_mix_g[l][None], w_cat, w_dt_hi, w_dt_lo, tm=512)

    y_ssd = _ssd(xbc, z, dt_raw, conv_w[l], conv_b[l][None], _pad_lanes(dt_bias[l][None]),
                 _pad_lanes(a_log[l][None]), jnp.repeat(d_skip[l], HEAD_DIM)[None], ssd_norm_g[l][None], bsz, s)
    y_att = _band_attn(q, k, v, _attn_bias_table(rel_bias[l]), attn_norm_g[l][None], bsz, s)

    wo = w_out[l].astype(BF16)
    tm_out = 256
    w_rt = jnp.concatenate([w_router_expert[l].T, w_router_group[l].T,
                            jnp.zeros((ROUTER_ROWS - N_EXPERTS - MOE_GROUPS, d), F32)], axis=0)
    wr_hi = w_rt.astype(BF16)
    wr_lo = (w_rt - wr_hi.astype(F32)).astype(BF16)
    b_rt = jnp.concatenate([b_router_expert[l], b_router_group[l],
                            jnp.zeros((ROUTER_ROWS - N_EXPERTS - MOE_GROUPS,), F32)])
    x1, h2_rows, route_i, route_f, counts = _out_router(
        y_ssd, y_att, x, mod, wo[:D_SSD], wo[D_SSD:], norm_ffn_g[l][None], wr_hi, wr_lo,
        jnp.broadcast_to(b_rt[:, None], (ROUTER_ROWS, tm_out)), tm=tm_out)

    n_blocks = t * TOP_K // SLOT_BLOCK + N_EXPERTS
    dest, blk = _slot_plan(route_i, counts, n_blocks, tp=2048)
    dest0, dest1 = dest[0], dest[1]
    block_e, used = blk[0, :n_blocks], blk[1, :1]

    xb = _dispatch(dest0, dest1, h2_rows, n_blocks * SLOT_BLOCK, tb=1024)
    yb = _experts(block_e, used, xb, w1[l], w3[l], w2[l])
    return _combine(dest0, dest1, yb, x1, route_f, mod, final_norm_g[None], bsz, s, tm=256)
```
